```python
import math
import jax
import jax.numpy as jnp
from jax import lax
import numpy as np

D_MODEL = 1024
BATCH = 8
SEQ = 2048
DEPTH = 4

GRID_W = 64
CTX_LEN = 256

MIX_W = D_MODEL
S5_W = D_MODEL // 4
S5_GH = 16
S5_G = S5_W // S5_GH
S5_P = 64
HEAD_DIM = 64
ATT_W = D_MODEL // 2
N_Q = ATT_W // HEAD_DIM
GQA_REP = 4
N_KV = N_Q // GQA_REP
ATT_KV = N_KV * HEAD_DIM
ROPE_PAIRS = HEAD_DIM // 4
ROPE_THETA = 10000.0
Q_BLOCK = 128
RW_W = D_MODEL // 4
RW_HD = 64
RW_H = RW_W // RW_HD
LORA_W = 32
LORA_A = 32
LORA_G = 64
RW_IN = 3 * RW_W + 2 * LORA_W + 2 * LORA_A + LORA_G
RW_LN_EPS = 64e-5
N_IN = S5_W + ATT_W + 2 * ATT_KV + RW_IN
IN_SPLITS = (S5_W, S5_W + ATT_W, S5_W + ATT_W + ATT_KV, S5_W + ATT_W + 2 * ATT_KV)
RW_SPLITS = (RW_W, 2 * RW_W, 3 * RW_W, 3 * RW_W + LORA_W, 3 * RW_W + 2 * LORA_W,
             3 * RW_W + 2 * LORA_W + LORA_A, 3 * RW_W + 2 * LORA_W + 2 * LORA_A)
D_FF = 2816
CONV_W = 3
N_MOD = 6
RMS_EPS = 1e-6

kernel_name = "hymba_s5_gqa_rwkv7_convffn_prefix_dit"

F32 = jnp.float32


def rms_norm(x, g):
    xf = x.astype(F32)
    y = xf * lax.rsqrt(jnp.mean(xf * xf, axis=-1, keepdims=True) + RMS_EPS)
    return y.astype(x.dtype) * g


def modulate(h, shift, scale):
    return h * (1.0 + scale) + shift


def _lin_rec_combine(left, right):
    a_l, b_l = left
    a_r, b_r = right
    return a_r * a_l, a_r * b_l + b_r


def s5_discretize(a_re, a_im, log_step, b_re, b_im):
    lam = lax.complex(a_re.astype(F32), a_im.astype(F32))
    step = jnp.exp(log_step.astype(F32))[:, None]
    lam_bar = jnp.exp(lam * step)
    b = lax.complex(b_re.astype(F32), b_im.astype(F32))
    b_bar = ((lam_bar - 1.0) / lam)[..., None] * b
    return lam_bar, b_bar


def s5_scan(u, lam_bar, b_bar, h0, reverse):
    bu = jnp.einsum('gph,blgh->blgp', b_bar, u.astype(jnp.complex64))
    if h0 is not None:
        first = -1 if reverse else 0
        bu = bu.at[:, first].add(lam_bar * h0)
    lam = jnp.broadcast_to(lam_bar, bu.shape)
    _, states = lax.associative_scan(_lin_rec_combine, (lam, bu), reverse=reverse, axis=1)
    return states


def s5_readout(c_mat, states):
    return jnp.real(jnp.einsum('ghp,blgp->blgh', c_mat, states))


def s5_glu(y, glu_w, glu_b, out_g):
    a = jax.nn.gelu(y)
    return rms_norm(a * jax.nn.sigmoid(a @ glu_w + glu_b), out_g)


def s5_mixer(u_lat, u_ctx, a_re, a_im, log_step, b_re, b_im, c_re, c_im, d, glu_w, glu_b,
             out_g, need_ctx_out):
    B, L, _ = u_lat.shape
    ul = u_lat.astype(F32).reshape(B, L, S5_G, S5_GH)
    uc = u_ctx.astype(F32).reshape(B, u_ctx.shape[1], S5_G, S5_GH)
    dd = d.astype(F32).reshape(S5_G, S5_GH)
    y_lat = ul * dd
    y_ctx = uc * dd if need_ctx_out else None
    for di, reverse in enumerate((False, True)):
        lam_bar, b_bar = s5_discretize(a_re[di], a_im[di], log_step[di], b_re[di], b_im[di])
        c_mat = lax.complex(c_re[di].astype(F32), c_im[di].astype(F32))
        st_c = s5_scan(uc, lam_bar, b_bar, None, reverse)
        h0 = st_c[:, 0] if reverse else st_c[:, -1]
        st_l = s5_scan(ul, lam_bar, b_bar, h0, reverse)
        y_lat = y_lat + s5_readout(c_mat, st_l)
        if need_ctx_out:
            y_ctx = y_ctx + s5_readout(c_mat, st_c)
    out_lat = s5_glu(y_lat.reshape(B, L, S5_W).astype(u_lat.dtype), glu_w, glu_b, out_g)
    out_ctx = None
    if need_ctx_out:
        out_ctx = s5_glu(y_ctx.reshape(B, u_ctx.shape[1], S5_W).astype(u_ctx.dtype),
                         glu_w, glu_b, out_g)
    return out_lat, out_ctx


def axial_rope(t, pos_row, pos_col):
    B, L, H, _ = t.shape
    inv = ROPE_THETA ** (-jnp.arange(ROPE_PAIRS, dtype=F32) / ROPE_PAIRS)
    ang = jnp.stack([pos_row[:, None] * inv, pos_col[:, None] * inv], axis=1)
    cos = jnp.cos(ang)[None, :, None]
    sin = jnp.sin(ang)[None, :, None]
    tf = t.astype(F32).reshape(B, L, H, 2, 2, ROPE_PAIRS)
    t1, t2 = tf[..., 0, :], tf[..., 1, :]
    out = jnp.stack([t1 * cos - t2 * sin, t2 * cos + t1 * sin], axis=-2)
    return out.reshape(B, L, H, HEAD_DIM).astype(t.dtype)


def block_attention(q, k, v):
    B, Lq = q.shape[0], q.shape[1]
    nb = Lq // Q_BLOCK
    qb = q.reshape(B, nb, Q_BLOCK, N_KV, GQA_REP, HEAD_DIM).transpose(1, 0, 2, 3, 4, 5)
    scale = HEAD_DIM ** -0.5

    def one_block(qi):
        s = jnp.einsum('bqgrd,bkgd->bgrqk', qi, k).astype(F32) * scale
        p = jax.nn.softmax(s, axis=-1).astype(v.dtype)
        return jnp.einsum('bgrqk,bkgd->bqgrd', p, v)

    o = lax.map(one_block, qb)
    return o.transpose(1, 0, 2, 3, 4, 5).reshape(B, Lq, N_Q * HEAD_DIM)


def attn_mixer(q_lat, k_lat, v_lat, q_ctx, k_ctx, v_ctx, qn_g, kn_g, out_g, pos_row, pos_col,
               need_ctx_out):
    def heads(t, n):
        return t.reshape(t.shape[0], t.shape[1], n, HEAD_DIM)
    ql = axial_rope(rms_norm(heads(q_lat, N_Q), qn_g), pos_row, pos_col)
    kl = axial_rope(rms_norm(heads(k_lat, N_KV), kn_g), pos_row, pos_col)
    kc = rms_norm(heads(k_ctx, N_KV), kn_g)
    vc = heads(v_ctx, N_KV)
    k_all = jnp.concatenate([kc, kl], axis=1)
    v_all = jnp.concatenate([vc, heads(v_lat, N_KV)], axis=1)
    out_lat = rms_norm(block_attention(ql, k_all, v_all), out_g)
    out_ctx = None
    if need_ctx_out:
        qc = rms_norm(heads(q_ctx, N_Q), qn_g)
        out_ctx = rms_norm(block_attention(qc, kc, vc), out_g)
    return out_lat, out_ctx


def centred_token_shift(z, mu):
    zp = jnp.pad(z, ((0, 0), (1, 1), (0, 0)))
    return z + mu * (0.5 * (zp[:, :-2] + zp[:, 2:]) - z)


def rwkv_inputs(z, w0, w2, a0, a2, g2, k_k, k_a):
    zf = z.astype(F32)
    B, L, _ = zf.shape
    r, k, v, wl_f, wl_b, al_f, al_b, gl = jnp.split(zf, RW_SPLITS, axis=-1)

    def hd(t):
        return t.reshape(B, L, RW_H, RW_HD)
    kk = hd(k * k_k)
    kk = kk / jnp.maximum(jnp.linalg.norm(kk, axis=-1, keepdims=True), 1e-12)
    dirs = []
    for di, (wl, al) in enumerate(((wl_f, al_f), (wl_b, al_b))):
        w_log = -jax.nn.softplus(-(w0[di] + jnp.tanh(wl) @ w2[di])) - 0.5
        decay = jnp.exp(-jnp.exp(w_log))
        iclr = jax.nn.sigmoid(a0[di] + al @ a2[di])
        k_dir = k * (1.0 + (iclr - 1.0) * k_a)
        dirs.append((hd(decay), hd(k_dir), hd(iclr)))
    g = jax.nn.sigmoid(gl) @ g2
    return hd(r), hd(v), kk, g, dirs


def wkv7_scan(r, decay, k, v, alpha, beta, s0, reverse):
    seq = tuple(jnp.swapaxes(t, 0, 1) for t in (r, decay, k, v, alpha, beta))

    def step(state, inp):
        r_t, w_t, k_t, v_t, a_t, b_t = inp
        sa = jnp.einsum('bhvk,bhk->bhv', state, a_t)
        state = (state * w_t[:, :, None, :] + sa[..., None] * b_t[:, :, None, :]
                 + v_t[..., None] * k_t[:, :, None, :])
        return state, jnp.einsum('bhvk,bhk->bhv', state, r_t)

    s_final, ys = lax.scan(step, s0, seq, reverse=reverse)
    return jnp.swapaxes(ys, 0, 1), s_final


def current_token_bonus(r, k, v, r_k):
    return jnp.sum(r * k * r_k, axis=-1, keepdims=True) * v


def rwkv_output(wkv, bonus, g, ln_g, ln_b, dtype):
    B, L = wkv.shape[0], wkv.shape[1]
    mean = jnp.mean(wkv, axis=-1, keepdims=True)
    var = jnp.mean(jnp.square(wkv - mean), axis=-1, keepdims=True)
    y = ((wkv - mean) * lax.rsqrt(var + RW_LN_EPS)).reshape(B, L, RW_W) * ln_g + ln_b
    y = y + bonus.reshape(B, L, RW_W)
    return (y * g).astype(dtype)


def rwkv_mixer(z_lat, z_ctx, mu, w0, w2, a0, a2, g2, k_k, k_a, r_k, ln_g, ln_b, need_ctx_out):
    r_l, v_l, kk_l, g_l, dirs_l = rwkv_inputs(centred_token_shift(z_lat, mu), w0, w2, a0, a2, g2, k_k, k_a)
    r_c, v_c, kk_c, g_c, dirs_c = rwkv_inputs(centred_token_shift(z_ctx, mu), w0, w2, a0, a2, g2, k_k, k_a)
    r_k = r_k.astype(F32)
    s0 = jnp.zeros((z_lat.shape[0], RW_H, RW_HD, RW_HD), F32)
    wkv_l, bon_l, wkv_c, bon_c = [], [], [], []
    for di, reverse in enumerate((False, True)):
        dec_c, kd_c, ic_c = dirs_c[di]
        y_c, s_c = wkv7_scan(r_c, dec_c, kd_c, v_c, -kk_c, kk_c * ic_c, s0, reverse)
        dec_l, kd_l, ic_l = dirs_l[di]
        y_l, _ = wkv7_scan(r_l, dec_l, kd_l, v_l, -kk_l, kk_l * ic_l, s_c, reverse)
        wkv_l.append(y_l)
        bon_l.append(current_token_bonus(r_l, kd_l, v_l, r_k))
        if need_ctx_out:
            wkv_c.append(y_c)
            bon_c.append(current_token_bonus(r_c, kd_c, v_c, r_k))
    out_lat = rwkv_output(wkv_l[0] + wkv_l[1], bon_l[0] + bon_l[1], g_l, ln_g, ln_b, z_lat.dtype)
    out_ctx = None
    if need_ctx_out:
        out_ctx = rwkv_output(wkv_c[0] + wkv_c[1], bon_c[0] + bon_c[1], g_c, ln_g, ln_b, z_ctx.dtype)
    return out_lat, out_ctx


def token_mixer(h_lat, h_ctx, w_in, w_out, s5_p, att_p, rw_p, pos_row, pos_col, need_ctx_out):
    zl = jnp.split(h_lat @ w_in, IN_SPLITS, axis=-1)
    zc = jnp.split(h_ctx @ w_in, IN_SPLITS, axis=-1)
    s5_l, s5_c = s5_mixer(zl[0], zc[0], *s5_p, need_ctx_out)
    at_l, at_c = attn_mixer(zl[1], zl[2], zl[3], zc[1], zc[2], zc[3], *att_p, pos_row, pos_col,
                            need_ctx_out)
    rw_l, rw_c = rwkv_mixer(zl[4], zc[4], *rw_p, need_ctx_out)
    o_lat = jnp.concatenate([s5_l, at_l, rw_l], axis=-1) @ w_out
    o_ctx = None
    if need_ctx_out:
        o_ctx = jnp.concatenate([s5_c, at_c, rw_c], axis=-1) @ w_out
    return o_lat, o_ctx


def centred_dwconv(x, w, b):
    C = x.shape[-1]
    y = lax.conv_general_dilated(x, w[:, None, :].astype(x.dtype), window_strides=(1,),
                                 padding=((CONV_W // 2, CONV_W // 2),),
                                 dimension_numbers=('NWC', 'WIO', 'NWC'),
                                 feature_group_count=C)
    return y + b


def conv_ffn(h, up, conv_w, conv_b, down):
    u = centred_dwconv(h @ up, conv_w, conv_b)
    gate, val = jnp.split(u, 2, axis=-1)
    return (jax.nn.silu(gate) * val) @ down


def setup_inputs(seed: int = 0) -> dict:
    key = jax.random.key(seed)
    ks = iter(jax.random.split(key, 48))
    L = DEPTH

    def nrm(shape, scale):
        return jax.random.normal(next(ks), shape, F32) * scale

    def unif(shape, lo, hi):
        return jax.random.uniform(next(ks), shape, F32, lo, hi)

    def gain(shape):
        return 1.0 + nrm(shape, 0.02)

    return {
        "x": nrm((BATCH, SEQ, D_MODEL), 1.0),
        "c": nrm((BATCH, D_MODEL), 1.0),
        "ctx": nrm((BATCH, CTX_LEN, D_MODEL), 1.0),
        "c_ctx": nrm((D_MODEL,), 1.0),
        "norm1_g": gain((L, D_MODEL)),
        "norm2_g": gain((L, D_MODEL)),
        "mod_w": nrm((L, D_MODEL, N_MOD * D_MODEL), 0.5 * D_MODEL ** -0.5),
        "mod_b": nrm((L, N_MOD * D_MODEL), 0.02),
        "w_in": nrm((L, D_MODEL, N_IN), D_MODEL ** -0.5),
        "w_out": nrm((L, MIX_W, D_MODEL), MIX_W ** -0.5),
        "s5_a_re": -0.5 + nrm((L, 2, S5_G, S5_P), 0.01),
        "s5_a_im": math.pi * jnp.arange(S5_P, dtype=F32) + nrm((L, 2, S5_G, S5_P), 0.01),
        "s5_log_step": unif((L, 2, S5_G), math.log(1e-3), math.log(1e-1)),
        "s5_b_re": nrm((L, 2, S5_G, S5_P, S5_GH), (2 * S5_GH) ** -0.5),
        "s5_b_im": nrm((L, 2, S5_G, S5_P, S5_GH), (2 * S5_GH) ** -0.5),
        "s5_c_re": nrm((L, 2, S5_G, S5_GH, S5_P), (2 * S5_P) ** -0.5),
        "s5_c_im": nrm((L, 2, S5_G, S5_GH, S5_P), (2 * S5_P) ** -0.5),
        "s5_d": nrm((L, S5_W), 1.0),
        "s5_glu_w": nrm((L, S5_W, S5_W), S5_W ** -0.5),
        "s5_glu_b": nrm((L, S5_W), 0.02),
        "s5_out_g": gain((L, S5_W)),
        "att_qn_g": gain((L, HEAD_DIM)),
        "att_kn_g": gain((L, HEAD_DIM)),
        "att_out_g": gain((L, ATT_W)),
        "rw_mu": unif((L, RW_IN), 0.0, 1.0),
        "rw_w0": unif((L, 2, RW_W), -5.0, -0.5),
        "rw_w2": nrm((L, 2, LORA_W, RW_W), 0.5 * LORA_W ** -0.5),
        "rw_a0": nrm((L, 2, RW_W), 0.1),
        "rw_a2": nrm((L, 2, LORA_A, RW_W), 0.5 * LORA_A ** -0.5),
        "rw_g2": nrm((L, LORA_G, RW_W), LORA_G ** -0.5),
        "rw_k_k": 0.85 + nrm((L, RW_W), 0.02),
        "rw_k_a": 1.0 + nrm((L, RW_W), 0.02),
        "rw_r_k": nrm((L, RW_H, RW_HD), 0.1),
        "rw_ln_g": gain((L, RW_W)),
        "rw_ln_b": nrm((L, RW_W), 0.02),
        "ffn_up": nrm((L, D_MODEL, 2 * D_FF), D_MODEL ** -0.5),
        "ffn_conv_w": nrm((L, CONV_W, 2 * D_FF), CONV_W ** -0.5),
        "ffn_conv_b": nrm((L, 2 * D_FF), 0.02),
        "ffn_down": nrm((L, D_FF, D_MODEL), D_FF ** -0.5),
        "final_g": gain((D_MODEL,)),
    }


def reference(x, c, ctx, c_ctx, norm1_g, norm2_g, mod_w, mod_b, w_in, w_out,
              s5_a_re, s5_a_im, s5_log_step, s5_b_re, s5_b_im, s5_c_re, s5_c_im, s5_d,
              s5_glu_w, s5_glu_b, s5_out_g, att_qn_g, att_kn_g, att_out_g,
              rw_mu, rw_w0, rw_w2, rw_a0, rw_a2, rw_g2, rw_k_k, rw_k_a, rw_r_k, rw_ln_g, rw_ln_b,
              ffn_up, ffn_conv_w, ffn_conv_b, ffn_down, final_g):
    n_tok = x.shape[1]
    rows = n_tok // GRID_W
    tok = jnp.arange(rows * GRID_W, dtype=jnp.int32)
    pos_row = (tok // GRID_W).astype(F32)
    pos_col = (tok % GRID_W).astype(F32)
    silu_c = jax.nn.silu(c)
    silu_cc = jax.nn.silu(c_ctx)
    xl, xc = x, ctx
    for l in range(DEPTH):
        ctx_continues = l < DEPTH - 1
        sh1, sc1, gt1, sh2, sc2, gt2 = jnp.split((silu_c @ mod_w[l] + mod_b[l])[:, None, :], N_MOD, axis=-1)
        csh1, csc1, cgt1, csh2, csc2, cgt2 = jnp.split(silu_cc @ mod_w[l] + mod_b[l], N_MOD, axis=-1)
        s5_p = (s5_a_re[l], s5_a_im[l], s5_log_step[l], s5_b_re[l], s5_b_im[l], s5_c_re[l],
                s5_c_im[l], s5_d[l], s5_glu_w[l], s5_glu_b[l], s5_out_g[l])
        att_p = (att_qn_g[l], att_kn_g[l], att_out_g[l])
        rw_p = (rw_mu[l], rw_w0[l], rw_w2[l], rw_a0[l], rw_a2[l], rw_g2[l], rw_k_k[l], rw_k_a[l],
                rw_r_k[l], rw_ln_g[l], rw_ln_b[l])
        hl = modulate(rms_norm(xl, norm1_g[l]), sh1, sc1)
        hc = modulate(rms_norm(xc, norm1_g[l]), csh1, csc1)
        ol, oc = token_mixer(hl, hc, w_in[l], w_out[l], s5_p, att_p, rw_p, pos_row, pos_col,
                             ctx_continues)
        xl = xl + gt1 * ol
        hl = modulate(rms_norm(xl, norm2_g[l]), sh2, sc2)
        xl = xl + gt2 * conv_ffn(hl, ffn_up[l], ffn_conv_w[l], ffn_conv_b[l], ffn_down[l])
        if ctx_continues:
            xc = xc + cgt1 * oc
            hc = modulate(rms_norm(xc, norm2_g[l]), csh2, csc2)
            xc = xc + cgt2 * conv_ffn(hc, ffn_up[l], ffn_conv_w[l], ffn_conv_b[l], ffn_down[l])
    return rms_norm(xl, final_g)
```

```python
import functools
import math

import jax
import jax.numpy as jnp
from jax import lax
from jax.experimental import pallas as pl
from jax.experimental.pallas import tpu as pltpu

F32 = jnp.float32
BF16 = jnp.bfloat16
HI = lax.Precision.HIGHEST

D_MODEL = 1024
GRID_W = 64
S5_W = 256
S5_GH = 16
S5_G = 16
S5_P = 64
HEAD_DIM = 64
ATT_W = 512
N_Q = 8
GQA_REP = 4
N_KV = 2
ATT_KV = 128
ROPE_PAIRS = 16
ROPE_THETA = 10000.0
RW_W = 256
RW_HD = 64
RW_H = 4
LORA_W = 32
LORA_A = 32
LORA_G = 64
RW_IN = 960
RW_PAD = 1024
RW_LN_EPS = 64e-5
D_FF = 2816
N_MOD = 6
RMS_EPS = 1e-6

S5_T = 32
V7X_VMEM_LIMIT = 56 * 1024 * 1024


def _params(sem, vmem=V7X_VMEM_LIMIT):
    return pltpu.CompilerParams(dimension_semantics=sem, vmem_limit_bytes=vmem)


def _rms(x):
    return x * lax.rsqrt(jnp.mean(x * x, axis=-1, keepdims=True) + RMS_EPS)


def _sigmoid(x):
    return 1.0 / (1.0 + jnp.exp(-x))


def _dot(a, b, precision=None):
    return jnp.dot(a, b, precision=precision, preferred_element_type=F32)


def _block_diag_mean(width, seg):
    i = jnp.arange(width)
    return ((i[:, None] // seg) == (i[None, :] // seg)).astype(F32) / seg


def _mod_kernel(c_ref, w_ref, b_ref, o_ref):
    c = c_ref[...]
    s = c * _sigmoid(c)
    o_ref[0] = _dot(s, w_ref[0], HI) + b_ref[0]


def _modulations(c, c_ctx, mod_w, mod_b):
    depth, d, n = mod_w.shape
    nb = c.shape[0]
    rows = ((nb + 1 + 7) // 8) * 8
    cc = jnp.zeros((rows, d), F32).at[:nb].set(c).at[nb].set(c_ctx)
    tn = n // 4
    out = pl.pallas_call(
        _mod_kernel,
        grid=(depth, n // tn),
        in_specs=[pl.BlockSpec((rows, d), lambda l, j: (0, 0)),
                  pl.BlockSpec((1, d, tn), lambda l, j: (l, 0, j)),
                  pl.BlockSpec((1, 1, tn), lambda l, j: (l, 0, j))],
        out_specs=pl.BlockSpec((1, rows, tn), lambda l, j: (l, 0, j)),
        out_shape=jax.ShapeDtypeStruct((depth, rows, n), F32),
        compiler_params=_params(("parallel", "parallel")),
    )(cc, mod_w, mod_b.reshape(depth, 1, n))
    lat = out[:, :nb]
    ctx = jnp.broadcast_to(out[:, nb:nb + 1], lat.shape)
    return jnp.stack([lat, ctx], axis=2)


def _pick_mod(mod, is_ctx, idx):
    lo = idx * D_MODEL
    return jnp.where(is_ctx, mod[1:2, lo:lo + D_MODEL], mod[0:1, lo:lo + D_MODEL])


def _inproj_kernel(x_ref, mod_ref, g_ref, ws_ref, wq_ref, wkv_ref, wr_ref,
                   s5_ref, q_ref, kv_ref, rw_ref, *, tm, ctx_len):
    j = pl.program_id(1)
    x = x_ref[0]
    row = j * tm + lax.broadcasted_iota(jnp.int32, (tm, 1), 0)
    is_ctx = row < ctx_len
    mod = mod_ref[0]
    h = _rms(x) * g_ref[...]
    h = h * (1.0 + _pick_mod(mod, is_ctx, 1)) + _pick_mod(mod, is_ctx, 0)
    hb = h.astype(BF16)
    s5_ref[0] = _dot(hb, ws_ref[...])
    q_ref[0] = _dot(hb, wq_ref[...])
    kv_ref[0] = _dot(hb, wkv_ref[...])
    rw_ref[0] = _dot(hb, wr_ref[...])


def _inproj(x, mod, g, w_in, tm, ctx_len):
    nb, nt, d = x.shape
    wb = w_in.astype(BF16)
    ws = wb[:, :S5_W]
    wq = wb[:, S5_W:S5_W + ATT_W]
    wkv = wb[:, S5_W + ATT_W:S5_W + ATT_W + 2 * ATT_KV]
    wr = jnp.pad(wb[:, S5_W + ATT_W + 2 * ATT_KV:], ((0, 0), (0, RW_PAD - RW_IN)))
    full = lambda a: pl.BlockSpec(a.shape, lambda b, j: (0,) * a.ndim)
    tok = lambda w: pl.BlockSpec((1, tm, w), lambda b, j: (b, j, 0))
    return pl.pallas_call(
        functools.partial(_inproj_kernel, tm=tm, ctx_len=ctx_len),
        grid=(nb, nt // tm),
        in_specs=[tok(d), pl.BlockSpec((1, 2, N_MOD * d), lambda b, j: (b, 0, 0)),
                  full(g), full(ws), full(wq), full(wkv), full(wr)],
        out_specs=[tok(S5_W), tok(ATT_W), tok(2 * ATT_KV), tok(RW_PAD)],
        out_shape=[jax.ShapeDtypeStruct((nb, nt, w), F32) for w in (S5_W, ATT_W, 2 * ATT_KV, RW_PAD)],
        compiler_params=_params(("parallel", "parallel")),
    )(x, mod, g, ws, wq, wkv, wr)


def _s5_operators(a_re, a_im, log_step, b_re, b_im, c_re, c_im, d):
    T = S5_T
    lam = lax.complex(a_re.astype(F32), a_im.astype(F32))
    step = jnp.exp(log_step.astype(F32))[..., None]
    lam_bar = jnp.exp(lam * step)
    bmat = lax.complex(b_re.astype(F32), b_im.astype(F32))
    b_bar = ((lam_bar - 1.0) / lam)[..., None] * bmat
    cmat = lax.complex(c_re.astype(F32), c_im.astype(F32))
    n = jnp.arange(T + 1, dtype=F32)
    pw = jnp.exp((lam * step)[..., None] * n)
    cp = jnp.swapaxes(cmat, -1, -2)[:, :, :, None, :] * pw[..., None]
    bp = jnp.transpose(pw, (0, 1, 3, 2))[:, :, :, None, :] * jnp.transpose(b_bar, (0, 1, 3, 2))[:, :, None]
    G, P, GH = S5_G, S5_P, S5_GH
    W = T * GH

    def ri_rows(z):
        return jnp.concatenate([jnp.real(z), -jnp.imag(z)], axis=1)

    def ri_cols(z):
        return jnp.concatenate([jnp.real(z), jnp.imag(z)], axis=2)

    bt = jnp.stack([jnp.concatenate([jnp.real(jnp.swapaxes(b_bar[di], -1, -2)),
                                     jnp.imag(jnp.swapaxes(b_bar[di], -1, -2))], axis=-1)
                    for di in range(2)], axis=1)
    cpk0 = ri_rows(cp[0, :, :, :T].reshape(G, P, W))
    cpk1 = ri_rows(cp[1, :, :, :T][:, :, ::-1].reshape(G, P, W))
    cc = jnp.concatenate([ri_rows(cp[0, :, :, 1:].reshape(G, P, W)),
                          ri_rows(cp[1, :, :, 1:][:, :, ::-1].reshape(G, P, W))], axis=1)
    bc = jnp.concatenate([ri_cols(bp[0, :, :T][:, ::-1].reshape(G, W, P)),
                          ri_cols(bp[1, :, :T].reshape(G, W, P))], axis=2)
    lt = pw[..., T]
    lam_rows = jnp.stack([jnp.concatenate([jnp.real(lt[0]), jnp.real(lt[0])], -1),
                          jnp.concatenate([-jnp.imag(lt[0]), jnp.imag(lt[0])], -1),
                          jnp.concatenate([jnp.real(lt[1]), jnp.real(lt[1])], -1),
                          jnp.concatenate([-jnp.imag(lt[1]), jnp.imag(lt[1])], -1)], axis=1)
    dvec = jnp.tile(d.astype(F32).reshape(G, 1, GH), (1, 1, T))
    return bt, cpk0, cpk1, bc, cc, lam_rows, dvec


def _s5_kernel(u_ref, bt_ref, cp0_ref, cp1_ref, bc_ref, cc_ref, lam_ref, d_ref, y_ref,
               kt_ref, sp_ref, *, n_ctx, n_chunks, nb):
    T, GH, P2 = S5_T, S5_GH, 2 * S5_P
    W = T * GH
    krow0 = _dot(bt_ref[0, 0], cp0_ref[0], HI)
    krow1 = _dot(bt_ref[0, 1], cp1_ref[0], HI)
    lane = lax.broadcasted_iota(jnp.int32, (GH, W), 1)
    for s in range(T):
        f = krow0 if s == 0 else pltpu.roll(krow0, s * GH, axis=1)
        f = jnp.where(lane >= s * GH, f, 0.0)
        sh = (T - 1 - s) * GH
        b = krow1 if sh == 0 else pltpu.roll(krow1, W - sh, axis=1)
        b = jnp.where(lane < (s + 1) * GH, b, 0.0)
        kt_ref[s * GH:(s + 1) * GH, :] = f + b
    u = u_ref[0]
    y = u * d_ref[0] + _dot(u, kt_ref[...], HI)
    e = _dot(u, bc_ref[0], HI)
    lam = lam_ref[0]

    def run(order, col, ra, rb):
        s = jnp.zeros((nb, P2), F32)
        for c in order:
            rows = slice(c * nb, (c + 1) * nb)
            sp_ref[rows, col:col + P2] = s
            s = ra * s + rb * pltpu.roll(s, S5_P, axis=1) + e[rows, col:col + P2]

    run(list(range(n_chunks)), 0, lam[0:1], lam[1:2])
    run(list(range(n_ctx - 1, -1, -1)) + list(range(n_chunks - 1, n_ctx - 1, -1)), P2, lam[2:3], lam[3:4])
    y_ref[0] = y + _dot(sp_ref[...], cc_ref[0], HI)


def _s5_scan(zs5, ops, ctx_len):
    nb, nt, _ = zs5.shape
    T, G, GH = S5_T, S5_G, S5_GH
    W = T * GH
    n_chunks = nt // T
    rows = n_chunks * nb
    u = zs5.reshape(nb, n_chunks, T, G, GH).transpose(3, 1, 0, 2, 4).reshape(G, rows, W)
    bt, cpk0, cpk1, bc, cc, lam_rows, dvec = ops
    per_g = lambda a: pl.BlockSpec((1,) + a.shape[1:], lambda g: (g,) + (0,) * (a.ndim - 1))
    y = pl.pallas_call(
        functools.partial(_s5_kernel, n_ctx=ctx_len // T, n_chunks=n_chunks, nb=nb),
        grid=(G,),
        in_specs=[per_g(u), per_g(bt), per_g(cpk0), per_g(cpk1), per_g(bc), per_g(cc),
                  per_g(lam_rows), per_g(dvec)],
        out_specs=per_g(u),
        out_shape=jax.ShapeDtypeStruct(u.shape, F32),
        scratch_shapes=[pltpu.VMEM((W, W), F32), pltpu.VMEM((rows, 4 * S5_P), F32)],
        compiler_params=_params(("parallel",)),
    )(u, bt, cpk0, cpk1, bc, cc, lam_rows, dvec)
    return y.reshape(G, n_chunks, nb, T, GH).transpose(2, 1, 3, 0, 4).reshape(nb, nt, S5_W)


def _rope_tables(nt, ctx_len, width):
    tok = jnp.arange(nt - ctx_len, dtype=jnp.int32)
    pos_row = (tok // GRID_W).astype(F32)
    pos_col = (tok % GRID_W).astype(F32)
    inv = ROPE_THETA ** (-jnp.arange(ROPE_PAIRS, dtype=F32) / ROPE_PAIRS)
    ang_r = pos_row[:, None] * inv
    ang_c = pos_col[:, None] * inv
    ang = jnp.concatenate([ang_r, ang_r, ang_c, ang_c], axis=1)
    sign = jnp.tile(jnp.concatenate([-jnp.ones(ROPE_PAIRS, F32), jnp.ones(ROPE_PAIRS, F32)]), 2)
    cos = jnp.concatenate([jnp.ones((ctx_len, HEAD_DIM), F32), jnp.cos(ang)], axis=0)
    sin = jnp.concatenate([jnp.zeros((ctx_len, HEAD_DIM), F32), jnp.sin(ang) * sign], axis=0)
    reps = width // HEAD_DIM
    return jnp.tile(cos, (1, reps)), jnp.tile(sin, (1, reps))


def _head_norm_rope(t, pmat, g, cos, sin):
    width = t.shape[-1]
    ms = _dot(t * t, pmat, HI)
    tn = t * lax.rsqrt(ms + RMS_EPS) * g
    lane = lax.broadcasted_iota(jnp.int32, tn.shape, 1)
    first = (lane % (2 * ROPE_PAIRS)) < ROPE_PAIRS
    partner = jnp.where(first, pltpu.roll(tn, width - ROPE_PAIRS, axis=1), pltpu.roll(tn, ROPE_PAIRS, axis=1))
    return tn * cos + partner * sin


def _attn_kernel(q_ref, kv_ref, cq_ref, sq_ref, ck_ref, sk_ref, qg_ref, kg_ref, og_ref, pq_ref, pk_ref,
                 o_ref, kn_ref, vb_ref, *, tq, ctx_len, nk):
    j = pl.program_id(1)

    @pl.when(j == 0)
    def _():
        kv = kv_ref[0]
        kr = _head_norm_rope(kv[:, :ATT_KV], pk_ref[...], kg_ref[...], ck_ref[...], sk_ref[...])
        for g in range(N_KV):
            kn_ref[g] = kr[:, g * HEAD_DIM:(g + 1) * HEAD_DIM].astype(BF16)
            vb_ref[g] = kv[:, ATT_KV + g * HEAD_DIM:ATT_KV + (g + 1) * HEAD_DIM].astype(BF16)

    qr = _head_norm_rope(q_ref[0], pq_ref[...], qg_ref[...], cq_ref[...], sq_ref[...]) * (HEAD_DIM ** -0.5)

    def attend(n_keys):
        outs = []
        for h in range(N_Q):
            g = h // GQA_REP
            qh = qr[:, h * HEAD_DIM:(h + 1) * HEAD_DIM].astype(BF16)
            s = lax.dot_general(qh, kn_ref[g, :n_keys, :], (((1,), (1,)), ((), ())),
                                preferred_element_type=F32)
            p = jnp.exp(s - jnp.max(s, axis=-1, keepdims=True))
            den = jnp.sum(p, axis=-1, keepdims=True)
            outs.append(_dot(p.astype(BF16), vb_ref[g, :n_keys, :]) / den)
        o = jnp.concatenate(outs, axis=1)
        o_ref[0] = _rms(o) * og_ref[...]

    @pl.when(j * tq < ctx_len)
    def _():
        attend(ctx_len)

    @pl.when(j * tq >= ctx_len)
    def _():
        attend(nk)


def _attention(zq, zkv, qn_g, kn_g, out_g, tq, ctx_len):
    nb, nt, _ = zq.shape
    cq, sq = _rope_tables(nt, ctx_len, ATT_W)
    ck, sk = cq[:, :ATT_KV], sq[:, :ATT_KV]
    qg = jnp.tile(qn_g, N_Q).reshape(1, ATT_W)
    kg = jnp.tile(kn_g, N_KV).reshape(1, ATT_KV)
    pq = _block_diag_mean(ATT_W, HEAD_DIM)
    pk = _block_diag_mean(ATT_KV, HEAD_DIM)
    full = lambda a: pl.BlockSpec(a.shape, lambda b, j: (0,) * a.ndim)
    return pl.pallas_call(
        functools.partial(_attn_kernel, tq=tq, ctx_len=ctx_len, nk=nt),
        grid=(nb, nt // tq),
        in_specs=[pl.BlockSpec((1, tq, ATT_W), lambda b, j: (b, j, 0)),
                  pl.BlockSpec((1, nt, 2 * ATT_KV), lambda b, j: (b, 0, 0)),
                  pl.BlockSpec((tq, ATT_W), lambda b, j: (j, 0)),
                  pl.BlockSpec((tq, ATT_W), lambda b, j: (j, 0)),
                  full(ck), full(sk), full(qg), full(kg),
                  pl.BlockSpec((1, ATT_W), lambda b, j: (0, 0)), full(pq), full(pk)],
        out_specs=pl.BlockSpec((1, tq, ATT_W), lambda b, j: (b, j, 0)),
        out_shape=jax.ShapeDtypeStruct((nb, nt, ATT_W), F32),
        scratch_shapes=[pltpu.VMEM((N_KV, nt, HEAD_DIM), BF16), pltpu.VMEM((N_KV, nt, HEAD_DIM), BF16)],
        compiler_params=_params(("parallel", "arbitrary")),
    )(zq, zkv, cq, sq, ck, sk, qg, kg, out_g.reshape(1, ATT_W), pq, pk)


def _rw_prep_kernel(z_ref, zp_ref, zn_ref, mu_ref, wl_ref, w0_ref, a0_ref, kk_ref, ka_ref, rk_ref, ones_ref,
                    r_ref, v_ref, a_ref, w_ref, b_ref, kd_ref, g_ref, bon_ref, *, tm, ctx_len, nt):
    j = pl.program_id(1)
    z = z_ref[0]
    row_l = lax.broadcasted_iota(jnp.int32, (tm, 1), 0)
    row = j * tm + row_l
    prev = jnp.where(row_l == 0, zp_ref[0, 7:8, :], pltpu.roll(z, 1, axis=0))
    prev = jnp.where((row == 0) | (row == ctx_len), 0.0, prev)
    nxt = jnp.where(row_l == tm - 1, zn_ref[0, 0:1, :], pltpu.roll(z, tm - 1, axis=0))
    nxt = jnp.where((row == ctx_len - 1) | (row == nt - 1), 0.0, nxt)
    zs = z + mu_ref[...] * (0.5 * (prev + nxt) - z)
    r = zs[:, 0:RW_W]
    k = zs[:, RW_W:2 * RW_W]
    v = zs[:, 2 * RW_W:3 * RW_W]
    lo = zs[:, 3 * RW_W:]
    lane = lax.broadcasted_iota(jnp.int32, lo.shape, 1)
    lo = jnp.where(lane < 2 * LORA_W, jnp.tanh(lo),
                   jnp.where(lane < 2 * LORA_W + 2 * LORA_A, lo, _sigmoid(lo)))
    proj = _dot(lo, wl_ref[...], HI)
    ones = ones_ref[...]
    kk = k * kk_ref[...]
    nrm = jnp.sqrt(_dot(kk * kk, ones, HI))
    kk = kk / jnp.maximum(nrm, 1e-12)
    r_ref[0] = r
    v_ref[0] = v
    a_ref[0] = -kk
    g_ref[0] = proj[:, 4 * RW_W:5 * RW_W]
    bon = jnp.zeros_like(r)
    for di in range(2):
        pre = w0_ref[di:di + 1, :] + proj[:, di * RW_W:(di + 1) * RW_W]
        w_log = -(jnp.maximum(-pre, 0.0) + jnp.log(1.0 + jnp.exp(-jnp.abs(pre)))) - 0.5
        w_ref[di, 0] = jnp.exp(-jnp.exp(w_log))
        iclr = _sigmoid(a0_ref[di:di + 1, :] + proj[:, (2 + di) * RW_W:(3 + di) * RW_W])
        kd = k * (1.0 + (iclr - 1.0) * ka_ref[...])
        kd_ref[di, 0] = kd
        b_ref[di, 0] = kk * iclr
        bon = bon + _dot(r * kd * rk_ref[...], ones, HI) * v
    bon_ref[0] = bon


def _rw_prep(zrw, mu, w0, w2, a0, a2, g2, k_k, k_a, r_k, tm, ctx_len):
    nb, nt, _ = zrw.shape
    wl = jnp.zeros((RW_PAD - 3 * RW_W, 5 * RW_W), F32)
    wl = wl.at[0:LORA_W, 0:RW_W].set(w2[0]).at[LORA_W:2 * LORA_W, RW_W:2 * RW_W].set(w2[1])
    o = 2 * LORA_W
    wl = wl.at[o:o + LORA_A, 2 * RW_W:3 * RW_W].set(a2[0]).at[o + LORA_A:o + 2 * LORA_A, 3 * RW_W:4 * RW_W].set(a2[1])
    o = 2 * LORA_W + 2 * LORA_A
    wl = wl.at[o:o + LORA_G, 4 * RW_W:5 * RW_W].set(g2)
    mu_p = jnp.pad(mu, (0, RW_PAD - RW_IN)).reshape(1, RW_PAD)
    ones = _block_diag_mean(RW_W, RW_HD) * RW_HD
    row = lambda a: a.reshape(1, RW_W)
    full = lambda a: pl.BlockSpec(a.shape, lambda b, j: (0,) * a.ndim)
    tok = pl.BlockSpec((1, tm, RW_W), lambda b, j: (b, j, 0))
    tok2 = pl.BlockSpec((2, 1, tm, RW_W), lambda b, j: (0, b, j, 0))
    one = jax.ShapeDtypeStruct((nb, nt, RW_W), F32)
    two = jax.ShapeDtypeStruct((2, nb, nt, RW_W), F32)
    n8 = nt // 8
    args = (zrw, zrw, zrw, mu_p, wl, w0, a0, row(k_k), row(k_a), row(r_k), ones)
    return pl.pallas_call(
        functools.partial(_rw_prep_kernel, tm=tm, ctx_len=ctx_len, nt=nt),
        grid=(nb, nt // tm),
        in_specs=[pl.BlockSpec((1, tm, RW_PAD), lambda b, j: (b, j, 0)),
                  pl.BlockSpec((1, 8, RW_PAD), lambda b, j: (b, jnp.maximum(j * (tm // 8) - 1, 0), 0)),
                  pl.BlockSpec((1, 8, RW_PAD), lambda b, j: (b, jnp.minimum((j + 1) * (tm // 8), n8 - 1), 0)),
                  ] + [full(a) for a in args[3:]],
        out_specs=[tok, tok, tok, tok2, tok2, tok2, tok, tok],
        out_shape=[one, one, one, two, two, two, one, one],
        compiler_params=_params(("parallel", "parallel")),
    )(*args)


def _rw_scan_kernel(kv_ref, v_ref, y_ref, s_ref, *, ts, nvh):
    @pl.when(pl.program_id(0) == 0)
    def _():
        s_ref[...] = jnp.zeros_like(s_ref)

    def step(i, carry):
        w = kv_ref[i, 0]
        a = kv_ref[i, 1]
        b = kv_ref[i, 2]
        kd = kv_ref[i, 3]
        r = kv_ref[i, 4]
        for vi in range(nvh):
            s = s_ref[vi]
            sa = jnp.sum(s * a, axis=0, keepdims=True)
            s = s * w + sa * b + v_ref[i, vi:vi + 1, :] * kd
            s_ref[vi] = s
            y_ref[i, vi:vi + 1, :] = jnp.sum(s * r, axis=0, keepdims=True)
        return carry

    lax.fori_loop(0, ts, step, 0)


def _seg_flip(x, ctx_len):
    return jnp.concatenate([x[:, :ctx_len][:, ::-1], x[:, ctx_len:][:, ::-1]], axis=1)


def _rw_scan(r, v, a, w, b, kd, ts, ctx_len):
    nb, nt, _ = r.shape
    nvh = RW_HD // 2

    def chains(x2):
        x2 = jnp.stack([x2[0], _seg_flip(x2[1], ctx_len)])
        t = x2.reshape(2, nb, nt, RW_H, RW_HD).transpose(2, 4, 0, 1, 3).reshape(nt, RW_HD, 2 * nb * RW_H)
        return jnp.concatenate([t, t], axis=-1)

    both = lambda x: jnp.stack([x, x])
    kv5 = jnp.stack([chains(w), chains(both(a)), chains(b), chains(kd), chains(both(r))], axis=1)
    v2 = jnp.stack([v, _seg_flip(v, ctx_len)])
    vv = v2.reshape(2, nb, nt, RW_H, 2, nvh).transpose(2, 5, 4, 0, 1, 3).reshape(nt, nvh, 4 * nb * RW_H)
    lanes = 4 * nb * RW_H
    y = pl.pallas_call(
        functools.partial(_rw_scan_kernel, ts=ts, nvh=nvh),
        grid=(nt // ts,),
        in_specs=[pl.BlockSpec((ts, 5, RW_HD, lanes), lambda i: (i, 0, 0, 0)),
                  pl.BlockSpec((ts, nvh, lanes), lambda i: (i, 0, 0))],
        out_specs=pl.BlockSpec((ts, nvh, lanes), lambda i: (i, 0, 0)),
        out_shape=jax.ShapeDtypeStruct((nt, nvh, lanes), F32),
        scratch_shapes=[pltpu.VMEM((nvh, RW_HD, lanes), F32)],
        compiler_params=_params(("arbitrary",)),
    )(kv5, vv)
    y = y.reshape(nt, nvh, 2, 2, nb, RW_H).transpose(3, 4, 0, 5, 2, 1).reshape(2, nb, nt, RW_W)
    return y[0], _seg_flip(y[1], ctx_len)


def _outproj_kernel(x_ref, mod_ref, ys_ref, at_ref, wf_ref, wb_ref, bon_ref, g_ref,
                    gw_ref, gb_ref, sg_ref, lg_ref, lb_ref, avg_ref, wo_ref, o_ref, *, tm, ctx_len):
    j = pl.program_id(1)
    row = j * tm + lax.broadcasted_iota(jnp.int32, (tm, 1), 0)
    is_ctx = row < ctx_len
    a = jax.nn.gelu(ys_ref[0])
    o1 = _rms(a * _sigmoid(_dot(a, gw_ref[...], HI) + gb_ref[...])) * sg_ref[...]
    wkv = wf_ref[0] + wb_ref[0]
    avg = avg_ref[...]
    cen = wkv - _dot(wkv, avg, HI)
    var = _dot(cen * cen, avg, HI)
    o3 = (cen * lax.rsqrt(var + RW_LN_EPS) * lg_ref[...] + lb_ref[...] + bon_ref[0]) * g_ref[0]
    o = (_dot(o1.astype(BF16), wo_ref[0:S5_W, :])
         + _dot(at_ref[0].astype(BF16), wo_ref[S5_W:S5_W + ATT_W, :])
         + _dot(o3.astype(BF16), wo_ref[S5_W + ATT_W:, :]))
    o_ref[0] = x_ref[0] + _pick_mod(mod_ref[0], is_ctx, 2) * o


def _outproj(x, mod, ys5, att, wkv_f, wkv_b, bonus, g, glu_w, glu_b, s5_out_g, ln_g, ln_b, w_out, tm, ctx_len):
    nb, nt, d = x.shape
    row = lambda a: a.reshape(1, -1)
    avg = _block_diag_mean(RW_W, RW_HD)
    params = (glu_w, row(glu_b), row(s5_out_g), row(ln_g), row(ln_b), avg, w_out.astype(BF16))
    full = lambda a: pl.BlockSpec(a.shape, lambda b, j: (0,) * a.ndim)
    tok = lambda w: pl.BlockSpec((1, tm, w), lambda b, j: (b, j, 0))
    return pl.pallas_call(
        functools.partial(_outproj_kernel, tm=tm, ctx_len=ctx_len),
        grid=(nb, nt // tm),
        in_specs=[tok(d), pl.BlockSpec((1, 2, N_MOD * d), lambda b, j: (b, 0, 0)),
                  tok(S5_W), tok(ATT_W), tok(RW_W), tok(RW_W), tok(RW_W), tok(RW_W)] + [full(a) for a in params],
        out_specs=tok(d),
        out_shape=jax.ShapeDtypeStruct(x.shape, F32),
        compiler_params=_params(("parallel", "parallel")),
    )(x, mod, ys5, att, wkv_f, wkv_b, bonus, g, *params)


FFN_HALO = 16


def _ffn_kernel(x_ref, xp_ref, xn_ref, mod_ref, g_ref, up_ref, cw_ref, cb_ref, dn_ref, o_ref,
                hs_ref, acc_ref, *, tm, tf, ctx_len, nt):
    j = pl.program_id(1)
    f = pl.program_id(2)
    hl = FFN_HALO
    mod = mod_ref[0]

    def adaln(xv, first_row):
        rows = first_row + lax.broadcasted_iota(jnp.int32, (xv.shape[0], 1), 0)
        is_ctx = rows < ctx_len
        h = _rms(xv) * g_ref[...]
        return (h * (1.0 + _pick_mod(mod, is_ctx, 4)) + _pick_mod(mod, is_ctx, 3)).astype(BF16)

    @pl.when(f == 0)
    def _():
        hs_ref[0:hl, :] = adaln(xp_ref[0], j * tm - hl)
        hs_ref[hl:hl + tm, :] = adaln(x_ref[0], j * tm)
        hs_ref[hl + tm:, :] = adaln(xn_ref[0], (j + 1) * tm)
        acc_ref[...] = jnp.zeros_like(acc_ref)

    u = _dot(hs_ref[...], up_ref[0])
    n_ext = tm + 2 * hl
    row = j * tm + lax.broadcasted_iota(jnp.int32, (tm, 1), 0)
    cen = u[hl:hl + tm]
    prev = pltpu.roll(u, 1, axis=0)[hl:hl + tm]
    prev = jnp.where((row == 0) | (row == ctx_len), 0.0, prev)
    nxt = pltpu.roll(u, n_ext - 1, axis=0)[hl:hl + tm]
    nxt = jnp.where((row == ctx_len - 1) | (row == nt - 1), 0.0, nxt)
    cw = cw_ref[0]
    c = cw[0:1] * prev + cw[1:2] * cen + cw[2:3] * nxt + cb_ref[0]
    gate = c[:, :tf]
    act = gate * _sigmoid(gate) * c[:, tf:]
    acc_ref[...] += _dot(act.astype(BF16), dn_ref[...])

    @pl.when(f == pl.num_programs(2) - 1)
    def _():
        rows = j * tm + lax.broadcasted_iota(jnp.int32, (tm, 1), 0)
        o_ref[0] = x_ref[0] + _pick_mod(mod, rows < ctx_len, 5) * acc_ref[...]


def _conv_ffn(x, mod, g, up, conv_w, conv_b, down, tm, tf, ctx_len):
    nb, nt, d = x.shape
    nf = D_FF // tf
    hl = FFN_HALO
    upb = up.astype(BF16)
    up_t = jnp.concatenate([upb[:, :D_FF].reshape(d, nf, tf), upb[:, D_FF:].reshape(d, nf, tf)], axis=2)
    up_t = up_t.transpose(1, 0, 2)
    pair = lambda a: jnp.concatenate([a[..., :D_FF].reshape(-1, nf, tf), a[..., D_FF:].reshape(-1, nf, tf)],
                                     axis=2).transpose(1, 0, 2)
    cw_t = pair(conv_w)
    cb_t = pair(conv_b.reshape(1, -1))
    nh = nt // hl
    return pl.pallas_call(
        functools.partial(_ffn_kernel, tm=tm, tf=tf, ctx_len=ctx_len, nt=nt),
        grid=(nb, nt // tm, nf),
        in_specs=[pl.BlockSpec((1, tm, d), lambda b, j, f: (b, j, 0)),
                  pl.BlockSpec((1, hl, d), lambda b, j, f: (b, jnp.maximum(j * (tm // hl) - 1, 0), 0)),
                  pl.BlockSpec((1, hl, d), lambda b, j, f: (b, jnp.minimum((j + 1) * (tm // hl), nh - 1), 0)),
                  pl.BlockSpec((1, 2, N_MOD * d), lambda b, j, f: (b, 0, 0)),
                  pl.BlockSpec((1, d), lambda b, j, f: (0, 0)),
                  pl.BlockSpec((1, d, 2 * tf), lambda b, j, f: (f, 0, 0)),
                  pl.BlockSpec((1, 3, 2 * tf), lambda b, j, f: (f, 0, 0)),
                  pl.BlockSpec((1, 1, 2 * tf), lambda b, j, f: (f, 0, 0)),
                  pl.BlockSpec((tf, d), lambda b, j, f: (f, 0))],
        out_specs=pl.BlockSpec((1, tm, d), lambda b, j, f: (b, j, 0)),
        out_shape=jax.ShapeDtypeStruct(x.shape, F32),
        scratch_shapes=[pltpu.VMEM((tm + 2 * hl, d), BF16), pltpu.VMEM((tm, d), F32)],
        compiler_params=_params(("parallel", "parallel", "arbitrary")),
    )(x, x, x, mod, g.reshape(1, d), up_t, cw_t, cb_t, down.astype(BF16))


def _final_kernel(x_ref, g_ref, o_ref):
    o_ref[0] = _rms(x_ref[0]) * g_ref[...]


def _final_norm(x, g, ctx_len, tm):
    nb, nt, d = x.shape
    off = ctx_len // tm
    return pl.pallas_call(
        _final_kernel,
        grid=(nb, (nt - ctx_len) // tm),
        in_specs=[pl.BlockSpec((1, tm, d), lambda b, j: (b, j + off, 0)),
                  pl.BlockSpec((1, d), lambda b, j: (0, 0))],
        out_specs=pl.BlockSpec((1, tm, d), lambda b, j: (b, j, 0)),
        out_shape=jax.ShapeDtypeStruct((nb, nt - ctx_len, d), F32),
        compiler_params=_params(("parallel", "parallel")),
    )(x, g.reshape(1, d))


def _forward(p, *, tm, tq, ts, tf, tfin):
    x, ctx = p["x"], p["ctx"]
    ctx_len = ctx.shape[1]
    depth = p["mod_w"].shape[0]
    xa = jnp.concatenate([ctx, x], axis=1)
    mods = _modulations(p["c"], p["c_ctx"], p["mod_w"], p["mod_b"])
    for l in range(depth):
        mod = mods[l]
        zs5, zq, zkv, zrw = _inproj(xa, mod, p["norm1_g"][l].reshape(1, -1), p["w_in"][l], tm, ctx_len)
        ops = _s5_operators(p["s5_a_re"][l], p["s5_a_im"][l], p["s5_log_step"][l], p["s5_b_re"][l],
                            p["s5_b_im"][l], p["s5_c_re"][l], p["s5_c_im"][l], p["s5_d"][l])
        ys5 = _s5_scan(zs5, ops, ctx_len)
        att = _attention(zq, zkv, p["att_qn_g"][l], p["att_kn_g"][l], p["att_out_g"][l], tq, ctx_len)
        r, v, a, w, b, kd, g, bonus = _rw_prep(zrw, p["rw_mu"][l], p["rw_w0"][l], p["rw_w2"][l], p["rw_a0"][l],
                                                 p["rw_a2"][l], p["rw_g2"][l], p["rw_k_k"][l], p["rw_k_a"][l],
                                                 p["rw_r_k"][l], tm, ctx_len)
        wkv_f, wkv_b = _rw_scan(r, v, a, w, b, kd, ts, ctx_len)
        xa = _outproj(xa, mod, ys5, att, wkv_f, wkv_b, bonus, g, p["s5_glu_w"][l], p["s5_glu_b"][l],
                      p["s5_out_g"][l], p["rw_ln_g"][l], p["rw_ln_b"][l], p["w_out"][l], tm, ctx_len)
        xa = _conv_ffn(xa, mod, p["norm2_g"][l], p["ffn_up"][l], p["ffn_conv_w"][l], p["ffn_conv_b"][l],
                       p["ffn_down"][l], tm, tf, ctx_len)
    return _final_norm(xa, p["final_g"], ctx_len, tfin)


_ARG_NAMES = ("x c ctx c_ctx norm1_g norm2_g mod_w mod_b w_in w_out s5_a_re s5_a_im s5_log_step s5_b_re s5_b_im "
              "s5_c_re s5_c_im s5_d s5_glu_w s5_glu_b s5_out_g att_qn_g att_kn_g att_out_g rw_mu rw_w0 rw_w2 "
              "rw_a0 rw_a2 rw_g2 rw_k_k rw_k_a rw_r_k rw_ln_g rw_ln_b ffn_up ffn_conv_w ffn_conv_b ffn_down "
              "final_g").split()


def kernel(x, c, ctx, c_ctx, norm1_g, norm2_g, mod_w, mod_b, w_in, w_out, s5_a_re, s5_a_im, s5_log_step, s5_b_re, s5_b_im, s5_c_re, s5_c_im, s5_d, s5_glu_w, s5_glu_b, s5_out_g, att_qn_g, att_kn_g, att_out_g, rw_mu, rw_w0, rw_w2, rw_a0, rw_a2, rw_g2, rw_k_k, rw_k_a, rw_r_k, rw_ln_g, rw_ln_b, ffn_up, ffn_conv_w, ffn_conv_b, ffn_down, final_g):
    args = (x, c, ctx, c_ctx, norm1_g, norm2_g, mod_w, mod_b, w_in, w_out, s5_a_re, s5_a_im, s5_log_step, s5_b_re,
            s5_b_im, s5_c_re, s5_c_im, s5_d, s5_glu_w, s5_glu_b, s5_out_g, att_qn_g, att_kn_g, att_out_g, rw_mu,
            rw_w0, rw_w2, rw_a0, rw_a2, rw_g2, rw_k_k, rw_k_a, rw_r_k, rw_ln_g, rw_ln_b, ffn_up, ffn_conv_w,
            ffn_conv_b, ffn_down, final_g)
    return _forward(dict(zip(_ARG_NAMES, args)), tm=768, tq=256, ts=32, tf=256, tfin=256)
```

```python
import functools
import math

import jax
import jax.numpy as jnp
from jax import lax
from jax.experimental import pallas as pl
from jax.experimental.pallas import tpu as pltpu

F32 = jnp.float32
BF16 = jnp.bfloat16
HI = lax.Precision.HIGHEST

D_MODEL = 1024
GRID_W = 64
S5_W = 256
S5_GH = 16
S5_G = 16
S5_P = 64
HEAD_DIM = 64
ATT_W = 512
N_Q = 8
GQA_REP = 4
N_KV = 2
ATT_KV = 128
ROPE_PAIRS = 16
ROPE_THETA = 10000.0
RW_W = 256
RW_HD = 64
RW_H = 4
LORA_W = 32
LORA_A = 32
LORA_G = 64
RW_IN = 960
RW_PAD = 1024
RW_LN_EPS = 64e-5
D_FF = 2816
N_MOD = 6
RMS_EPS = 1e-6

S5_T = 32
V7X_VMEM_LIMIT = 56 * 1024 * 1024


def _params(sem, vmem=V7X_VMEM_LIMIT):
    return pltpu.CompilerParams(dimension_semantics=sem, vmem_limit_bytes=vmem)


def _rms(x):
    return x * lax.rsqrt(jnp.mean(x * x, axis=-1, keepdims=True) + RMS_EPS)


def _sigmoid(x):
    return 1.0 / (1.0 + jnp.exp(-x))


def _dot(a, b, precision=None):
    return jnp.dot(a, b, precision=precision, preferred_element_type=F32)


def _block_diag_mean(width, seg):
    i = jnp.arange(width)
    return ((i[:, None] // seg) == (i[None, :] // seg)).astype(F32) / seg


def _mod_kernel(c_ref, w_ref, b_ref, o_ref):
    c = c_ref[...]
    s = c * _sigmoid(c)
    o_ref[0] = _dot(s, w_ref[0], HI) + b_ref[0]


def _modulations(c, c_ctx, mod_w, mod_b):
    depth, d, n = mod_w.shape
    nb = c.shape[0]
    rows = ((nb + 1 + 7) // 8) * 8
    cc = jnp.zeros((rows, d), F32).at[:nb].set(c).at[nb].set(c_ctx)
    tn = n // 4
    out = pl.pallas_call(
        _mod_kernel,
        grid=(depth, n // tn),
        in_specs=[pl.BlockSpec((rows, d), lambda l, j: (0, 0)),
                  pl.BlockSpec((1, d, tn), lambda l, j: (l, 0, j)),
                  pl.BlockSpec((1, 1, tn), lambda l, j: (l, 0, j))],
        out_specs=pl.BlockSpec((1, rows, tn), lambda l, j: (l, 0, j)),
        out_shape=jax.ShapeDtypeStruct((depth, rows, n), F32),
        compiler_params=_params(("parallel", "parallel")),
    )(cc, mod_w, mod_b.reshape(depth, 1, n))
    lat = out[:, :nb]
    ctx = jnp.broadcast_to(out[:, nb:nb + 1], lat.shape)
    return jnp.stack([lat, ctx], axis=2)


def _pick_mod(mod, is_ctx, idx):
    lo = idx * D_MODEL
    return jnp.where(is_ctx, mod[1:2, lo:lo + D_MODEL], mod[0:1, lo:lo + D_MODEL])


def _inproj_kernel(x_ref, mod_ref, g_ref, ws_ref, wq_ref, wkv_ref, wr_ref,
                   s5_ref, q_ref, kv_ref, rw_ref, *, tm, ctx_len):
    j = pl.program_id(1)
    x = x_ref[0]
    row = j * tm + lax.broadcasted_iota(jnp.int32, (tm, 1), 0)
    is_ctx = row < ctx_len
    mod = mod_ref[0]
    h = _rms(x) * g_ref[...]
    h = h * (1.0 + _pick_mod(mod, is_ctx, 1)) + _pick_mod(mod, is_ctx, 0)
    hb = h.astype(BF16)
    s5_ref[0] = _dot(hb, ws_ref[...])
    q_ref[0] = _dot(hb, wq_ref[...])
    kv_ref[0] = _dot(hb, wkv_ref[...])
    rw_ref[0] = _dot(hb, wr_ref[...])


def _inproj(x, mod, g, w_in, tm, ctx_len):
    nb, nt, d = x.shape
    wb = w_in.astype(BF16)
    ws = wb[:, :S5_W]
    wq = wb[:, S5_W:S5_W + ATT_W]
    wkv = wb[:, S5_W + ATT_W:S5_W + ATT_W + 2 * ATT_KV]
    wr = jnp.pad(wb[:, S5_W + ATT_W + 2 * ATT_KV:], ((0, 0), (0, RW_PAD - RW_IN)))
    full = lambda a: pl.BlockSpec(a.shape, lambda b, j: (0,) * a.ndim)
    tok = lambda w: pl.BlockSpec((1, tm, w), lambda b, j: (b, j, 0))
    return pl.pallas_call(
        functools.partial(_inproj_kernel, tm=tm, ctx_len=ctx_len),
        grid=(nb, nt // tm),
        in_specs=[tok(d), pl.BlockSpec((1, 2, N_MOD * d), lambda b, j: (b, 0, 0)),
                  full(g), full(ws), full(wq), full(wkv), full(wr)],
        out_specs=[tok(S5_W), tok(ATT_W), tok(2 * ATT_KV), tok(RW_PAD)],
        out_shape=[jax.ShapeDtypeStruct((nb, nt, w), F32) for w in (S5_W, ATT_W, 2 * ATT_KV, RW_PAD)],
        compiler_params=_params(("parallel", "parallel")),
    )(x, mod, g, ws, wq, wkv, wr)


def _s5_operators(a_re, a_im, log_step, b_re, b_im, c_re, c_im, d):
    T = S5_T
    lam = lax.complex(a_re.astype(F32), a_im.astype(F32))
    step = jnp.exp(log_step.astype(F32))[..., None]
    lam_bar = jnp.exp(lam * step)
    bmat = lax.complex(b_re.astype(F32), b_im.astype(F32))
    b_bar = ((lam_bar - 1.0) / lam)[..., None] * bmat
    cmat = lax.complex(c_re.astype(F32), c_im.astype(F32))
    n = jnp.arange(T + 1, dtype=F32)
    pw = jnp.exp((lam * step)[..., None] * n)
    cp = jnp.swapaxes(cmat, -1, -2)[:, :, :, None, :] * pw[..., None]
    bp = jnp.transpose(pw, (0, 1, 3, 2))[:, :, :, None, :] * jnp.transpose(b_bar, (0, 1, 3, 2))[:, :, None]
    G, P, GH = S5_G, S5_P, S5_GH
    W = T * GH

    def ri_rows(z):
        return jnp.concatenate([jnp.real(z), -jnp.imag(z)], axis=1)

    def ri_cols(z):
        return jnp.concatenate([jnp.real(z), jnp.imag(z)], axis=2)

    bt = jnp.stack([jnp.concatenate([jnp.real(jnp.swapaxes(b_bar[di], -1, -2)),
                                     jnp.imag(jnp.swapaxes(b_bar[di], -1, -2))], axis=-1)
                    for di in range(2)], axis=1)
    cpk0 = ri_rows(cp[0, :, :, :T].reshape(G, P, W))
    cpk1 = ri_rows(cp[1, :, :, :T][:, :, ::-1].reshape(G, P, W))
    cc = jnp.concatenate([ri_rows(cp[0, :, :, 1:].reshape(G, P, W)),
                          ri_rows(cp[1, :, :, 1:][:, :, ::-1].reshape(G, P, W))], axis=1)
    bc = jnp.concatenate([ri_cols(bp[0, :, :T][:, ::-1].reshape(G, W, P)),
                          ri_cols(bp[1, :, :T].reshape(G, W, P))], axis=2)
    lt = pw[..., T]
    lam_rows = jnp.stack([jnp.concatenate([jnp.real(lt[0]), jnp.real(lt[0])], -1),
                          jnp.concatenate([-jnp.imag(lt[0]), jnp.imag(lt[0])], -1),
                          jnp.concatenate([jnp.real(lt[1]), jnp.real(lt[1])], -1),
                          jnp.concatenate([-jnp.imag(lt[1]), jnp.imag(lt[1])], -1)], axis=1)
    dvec = jnp.tile(d.astype(F32).reshape(G, 1, GH), (1, 1, T))
    return bt, cpk0, cpk1, bc, cc, lam_rows, dvec


def _s5_kernel(u_ref, bt_ref, cp0_ref, cp1_ref, bc_ref, cc_ref, lam_ref, d_ref, y_ref,
               kt_ref, sp_ref, *, n_ctx, n_chunks, nb):
    T, GH, P2 = S5_T, S5_GH, 2 * S5_P
    W = T * GH
    krow0 = _dot(bt_ref[0, 0], cp0_ref[0], HI)
    krow1 = _dot(bt_ref[0, 1], cp1_ref[0], HI)
    lane = lax.broadcasted_iota(jnp.int32, (GH, W), 1)
    for s in range(T):
        f = krow0 if s == 0 else pltpu.roll(krow0, s * GH, axis=1)
        f = jnp.where(lane >= s * GH, f, 0.0)
        sh = (T - 1 - s) * GH
        b = krow1 if sh == 0 else pltpu.roll(krow1, W - sh, axis=1)
        b = jnp.where(lane < (s + 1) * GH, b, 0.0)
        kt_ref[s * GH:(s + 1) * GH, :] = f + b
    u = u_ref[0]
    y = u * d_ref[0] + _dot(u, kt_ref[...], HI)
    e = _dot(u, bc_ref[0], HI)
    lam = lam_ref[0]

    def run(order, col, ra, rb):
        s = jnp.zeros((nb, P2), F32)
        for c in order:
            rows = slice(c * nb, (c + 1) * nb)
            sp_ref[rows, col:col + P2] = s
            s = ra * s + rb * pltpu.roll(s, S5_P, axis=1) + e[rows, col:col + P2]

    run(list(range(n_chunks)), 0, lam[0:1], lam[1:2])
    run(list(range(n_ctx - 1, -1, -1)) + list(range(n_chunks - 1, n_ctx - 1, -1)), P2, lam[2:3], lam[3:4])
    y_ref[0] = y + _dot(sp_ref[...], cc_ref[0], HI)


def _s5_scan(zs5, ops, ctx_len):
    nb, nt, _ = zs5.shape
    T, G, GH = S5_T, S5_G, S5_GH
    W = T * GH
    n_chunks = nt // T
    rows = n_chunks * nb
    u = zs5.reshape(nb, n_chunks, T, G, GH).transpose(3, 1, 0, 2, 4).reshape(G, rows, W)
    bt, cpk0, cpk1, bc, cc, lam_rows, dvec = ops
    per_g = lambda a: pl.BlockSpec((1,) + a.shape[1:], lambda g: (g,) + (0,) * (a.ndim - 1))
    y = pl.pallas_call(
        functools.partial(_s5_kernel, n_ctx=ctx_len // T, n_chunks=n_chunks, nb=nb),
        grid=(G,),
        in_specs=[per_g(u), per_g(bt), per_g(cpk0), per_g(cpk1), per_g(bc), per_g(cc),
                  per_g(lam_rows), per_g(dvec)],
        out_specs=per_g(u),
        out_shape=jax.ShapeDtypeStruct(u.shape, F32),
        scratch_shapes=[pltpu.VMEM((W, W), F32), pltpu.VMEM((rows, 4 * S5_P), F32)],
        compiler_params=_params(("parallel",)),
    )(u, bt, cpk0, cpk1, bc, cc, lam_rows, dvec)
    return y.reshape(G, n_chunks, nb, T, GH).transpose(2, 1, 3, 0, 4).reshape(nb, nt, S5_W)


def _rope_tables(nt, ctx_len, width):
    tok = jnp.arange(nt - ctx_len, dtype=jnp.int32)
    pos_row = (tok // GRID_W).astype(F32)
    pos_col = (tok % GRID_W).astype(F32)
    inv = ROPE_THETA ** (-jnp.arange(ROPE_PAIRS, dtype=F32) / ROPE_PAIRS)
    ang_r = pos_row[:, None] * inv
    ang_c = pos_col[:, None] * inv
    ang = jnp.concatenate([ang_r, ang_r, ang_c, ang_c], axis=1)
    sign = jnp.tile(jnp.concatenate([-jnp.ones(ROPE_PAIRS, F32), jnp.ones(ROPE_PAIRS, F32)]), 2)
    cos = jnp.concatenate([jnp.ones((ctx_len, HEAD_DIM), F32), jnp.cos(ang)], axis=0)
    sin = jnp.concatenate([jnp.zeros((ctx_len, HEAD_DIM), F32), jnp.sin(ang) * sign], axis=0)
    reps = width // HEAD_DIM
    return jnp.tile(cos, (1, reps)), jnp.tile(sin, (1, reps))


def _head_norm_rope(t, pmat, g, cos, sin):
    width = t.shape[-1]
    ms = _dot(t * t, pmat, HI)
    tn = t * lax.rsqrt(ms + RMS_EPS) * g
    lane = lax.broadcasted_iota(jnp.int32, tn.shape, 1)
    first = (lane % (2 * ROPE_PAIRS)) < ROPE_PAIRS
    partner = jnp.where(first, pltpu.roll(tn, width - ROPE_PAIRS, axis=1), pltpu.roll(tn, ROPE_PAIRS, axis=1))
    return tn * cos + partner * sin


def _attn_kernel(q_ref, kv_ref, cq_ref, sq_ref, ck_ref, sk_ref, qg_ref, kg_ref, og_ref, pq_ref, pk_ref,
                 o_ref, kn_ref, vb_ref, *, tq, ctx_len, nk):
    j = pl.program_id(1)

    @pl.when(j == 0)
    def _():
        kv = kv_ref[0]
        kr = _head_norm_rope(kv[:, :ATT_KV], pk_ref[...], kg_ref[...], ck_ref[...], sk_ref[...])
        for g in range(N_KV):
            kn_ref[g] = kr[:, g * HEAD_DIM:(g + 1) * HEAD_DIM].astype(BF16)
            vb_ref[g] = kv[:, ATT_KV + g * HEAD_DIM:ATT_KV + (g + 1) * HEAD_DIM].astype(BF16)

    qr = _head_norm_rope(q_ref[0], pq_ref[...], qg_ref[...], cq_ref[...], sq_ref[...]) * (HEAD_DIM ** -0.5)

    def attend(n_keys):
        outs = []
        for h in range(N_Q):
            g = h // GQA_REP
            qh = qr[:, h * HEAD_DIM:(h + 1) * HEAD_DIM].astype(BF16)
            s = lax.dot_general(qh, kn_ref[g, :n_keys, :], (((1,), (1,)), ((), ())),
                                preferred_element_type=F32)
            p = jnp.exp(s - jnp.max(s, axis=-1, keepdims=True))
            den = jnp.sum(p, axis=-1, keepdims=True)
            outs.append(_dot(p.astype(BF16), vb_ref[g, :n_keys, :]) / den)
        o = jnp.concatenate(outs, axis=1)
        o_ref[0] = _rms(o) * og_ref[...]

    @pl.when(j * tq < ctx_len)
    def _():
        attend(ctx_len)

    @pl.when(j * tq >= ctx_len)
    def _():
        attend(nk)


def _attention(zq, zkv, qn_g, kn_g, out_g, tq, ctx_len):
    nb, nt, _ = zq.shape
    cq, sq = _rope_tables(nt, ctx_len, ATT_W)
    ck, sk = cq[:, :ATT_KV], sq[:, :ATT_KV]
    qg = jnp.tile(qn_g, N_Q).reshape(1, ATT_W)
    kg = jnp.tile(kn_g, N_KV).reshape(1, ATT_KV)
    pq = _block_diag_mean(ATT_W, HEAD_DIM)
    pk = _block_diag_mean(ATT_KV, HEAD_DIM)
    full = lambda a: pl.BlockSpec(a.shape, lambda b, j: (0,) * a.ndim)
    return pl.pallas_call(
        functools.partial(_attn_kernel, tq=tq, ctx_len=ctx_len, nk=nt),
        grid=(nb, nt // tq),
        in_specs=[pl.BlockSpec((1, tq, ATT_W), lambda b, j: (b, j, 0)),
                  pl.BlockSpec((1, nt, 2 * ATT_KV), lambda b, j: (b, 0, 0)),
                  pl.BlockSpec((tq, ATT_W), lambda b, j: (j, 0)),
                  pl.BlockSpec((tq, ATT_W), lambda b, j: (j, 0)),
                  full(ck), full(sk), full(qg), full(kg),
                  pl.BlockSpec((1, ATT_W), lambda b, j: (0, 0)), full(pq), full(pk)],
        out_specs=pl.BlockSpec((1, tq, ATT_W), lambda b, j: (b, j, 0)),
        out_shape=jax.ShapeDtypeStruct((nb, nt, ATT_W), F32),
        scratch_shapes=[pltpu.VMEM((N_KV, nt, HEAD_DIM), BF16), pltpu.VMEM((N_KV, nt, HEAD_DIM), BF16)],
        compiler_params=_params(("parallel", "arbitrary")),
    )(zq, zkv, cq, sq, ck, sk, qg, kg, out_g.reshape(1, ATT_W), pq, pk)


def _rw_prep_kernel(z_ref, zp_ref, zn_ref, mu_ref, wl_ref, w0_ref, a0_ref, kk_ref, ka_ref, rk_ref, ones_ref,
                    r_ref, v_ref, a_ref, w_ref, b_ref, kd_ref, g_ref, bon_ref, *, tm, ctx_len, nt):
    j = pl.program_id(1)
    z = z_ref[0]
    row_l = lax.broadcasted_iota(jnp.int32, (tm, 1), 0)
    row = j * tm + row_l
    prev = jnp.where(row_l == 0, zp_ref[0, 7:8, :], pltpu.roll(z, 1, axis=0))
    prev = jnp.where((row == 0) | (row == ctx_len), 0.0, prev)
    nxt = jnp.where(row_l == tm - 1, zn_ref[0, 0:1, :], pltpu.roll(z, tm - 1, axis=0))
    nxt = jnp.where((row == ctx_len - 1) | (row == nt - 1), 0.0, nxt)
    zs = z + mu_ref[...] * (0.5 * (prev + nxt) - z)
    r = zs[:, 0:RW_W]
    k = zs[:, RW_W:2 * RW_W]
    v = zs[:, 2 * RW_W:3 * RW_W]
    lo = zs[:, 3 * RW_W:]
    lane = lax.broadcasted_iota(jnp.int32, lo.shape, 1)
    lo = jnp.where(lane < 2 * LORA_W, jnp.tanh(lo),
                   jnp.where(lane < 2 * LORA_W + 2 * LORA_A, lo, _sigmoid(lo)))
    proj = _dot(lo, wl_ref[...], HI)
    ones = ones_ref[...]
    kk = k * kk_ref[...]
    nrm = jnp.sqrt(_dot(kk * kk, ones, HI))
    kk = kk / jnp.maximum(nrm, 1e-12)
    r_ref[0] = r
    v_ref[0] = v
    a_ref[0] = -kk
    g_ref[0] = proj[:, 4 * RW_W:5 * RW_W]
    bon = jnp.zeros_like(r)
    for di in range(2):
        pre = w0_ref[di:di + 1, :] + proj[:, di * RW_W:(di + 1) * RW_W]
        w_log = -(jnp.maximum(-pre, 0.0) + jnp.log(1.0 + jnp.exp(-jnp.abs(pre)))) - 0.5
        w_ref[di, 0] = jnp.exp(-jnp.exp(w_log))
        iclr = _sigmoid(a0_ref[di:di + 1, :] + proj[:, (2 + di) * RW_W:(3 + di) * RW_W])
        kd = k * (1.0 + (iclr - 1.0) * ka_ref[...])
        kd_ref[di, 0] = kd
        b_ref[di, 0] = kk * iclr
        bon = bon + _dot(r * kd * rk_ref[...], ones, HI) * v
    bon_ref[0] = bon


def _rw_prep(zrw, mu, w0, w2, a0, a2, g2, k_k, k_a, r_k, tm, ctx_len):
    nb, nt, _ = zrw.shape
    wl = jnp.zeros((RW_PAD - 3 * RW_W, 5 * RW_W), F32)
    wl = wl.at[0:LORA_W, 0:RW_W].set(w2[0]).at[LORA_W:2 * LORA_W, RW_W:2 * RW_W].set(w2[1])
    o = 2 * LORA_W
    wl = wl.at[o:o + LORA_A, 2 * RW_W:3 * RW_W].set(a2[0]).at[o + LORA_A:o + 2 * LORA_A, 3 * RW_W:4 * RW_W].set(a2[1])
    o = 2 * LORA_W + 2 * LORA_A
    wl = wl.at[o:o + LORA_G, 4 * RW_W:5 * RW_W].set(g2)
    mu_p = jnp.pad(mu, (0, RW_PAD - RW_IN)).reshape(1, RW_PAD)
    ones = _block_diag_mean(RW_W, RW_HD) * RW_HD
    row = lambda a: a.reshape(1, RW_W)
    full = lambda a: pl.BlockSpec(a.shape, lambda b, j: (0,) * a.ndim)
    tok = pl.BlockSpec((1, tm, RW_W), lambda b, j: (b, j, 0))
    tok2 = pl.BlockSpec((2, 1, tm, RW_W), lambda b, j: (0, b, j, 0))
    one = jax.ShapeDtypeStruct((nb, nt, RW_W), F32)
    two = jax.ShapeDtypeStruct((2, nb, nt, RW_W), F32)
    n8 = nt // 8
    args = (zrw, zrw, zrw, mu_p, wl, w0, a0, row(k_k), row(k_a), row(r_k), ones)
    return pl.pallas_call(
        functools.partial(_rw_prep_kernel, tm=tm, ctx_len=ctx_len, nt=nt),
        grid=(nb, nt // tm),
        in_specs=[pl.BlockSpec((1, tm, RW_PAD), lambda b, j: (b, j, 0)),
                  pl.BlockSpec((1, 8, RW_PAD), lambda b, j: (b, jnp.maximum(j * (tm // 8) - 1, 0), 0)),
                  pl.BlockSpec((1, 8, RW_PAD), lambda b, j: (b, jnp.minimum((j + 1) * (tm // 8), n8 - 1), 0)),
                  ] + [full(a) for a in args[3:]],
        out_specs=[tok, tok, tok, tok2, tok2, tok2, tok, tok],
        out_shape=[one, one, one, two, two, two, one, one],
        compiler_params=_params(("parallel", "parallel")),
    )(*args)


RW_KH = RW_HD // 2


def _to_chains_kernel(xa_ref, xb_ref, o_ref, scr_ref, *, nb, tt):
    for j, ref in enumerate((xa_ref, xb_ref)):
        for b in range(nb):
            xt = ref[0, b].T
            for h in range(RW_H):
                for kh in range(2):
                    src = (h * 2 + kh) * RW_KH
                    dst = (((kh * 2 + j) * nb + b) * RW_H + h) * RW_KH
                    scr_ref[dst:dst + RW_KH, :] = xt[src:src + RW_KH, :]
    lanes = 4 * nb * RW_H
    for k in range(RW_KH):
        o_ref[pl.ds(k, tt, stride=RW_KH), :] = scr_ref[pl.ds(k, lanes, stride=RW_KH), :].T


def _to_chains(xa, da, xb, db, tt):
    _, nb, nt, _ = xa.shape
    lanes = 4 * nb * RW_H
    return pl.pallas_call(
        functools.partial(_to_chains_kernel, nb=nb, tt=tt),
        grid=(nt // tt,),
        in_specs=[pl.BlockSpec((1, nb, tt, RW_W), lambda i: (da, 0, i, 0)),
                  pl.BlockSpec((1, nb, tt, RW_W), lambda i: (db, 0, i, 0))],
        out_specs=pl.BlockSpec((tt * RW_KH, lanes), lambda i: (i, 0)),
        out_shape=jax.ShapeDtypeStruct((nt * RW_KH, lanes), F32),
        scratch_shapes=[pltpu.VMEM((lanes * RW_KH, tt), F32)],
        compiler_params=_params(("parallel",)),
    )(xa, xb)


def _rw_scan_kernel(fa_ref, fb_ref, fc_ref, ba_ref, bb_ref, bc_ref, yf_ref, yb_ref, s_ref, *, ts, lanes):
    @pl.when(pl.program_id(0) == 0)
    def _():
        s_ref[...] = jnp.zeros_like(s_ref)

    lane = lax.broadcasted_iota(jnp.int32, (RW_KH, lanes), 1)
    q = lanes // 4
    slot0 = (lane // q) % 2 == 0
    low = lane < 2 * q

    def step(i, carry):
        fr = pl.ds(pl.multiple_of(i * RW_KH, RW_KH), RW_KH)
        br = pl.ds(pl.multiple_of((ts - 1 - i) * RW_KH, RW_KH), RW_KH)

        def split(f_ref, b_ref):
            f = f_ref[fr, :]
            b = b_ref[br, :]
            return (jnp.where(slot0, f, pltpu.roll(b, q, axis=1)),
                    jnp.where(slot0, pltpu.roll(f, lanes - q, axis=1), b))

        w, bt = split(fa_ref, ba_ref)
        kd, a = split(fb_ref, bb_ref)
        r, v = split(fc_ref, bc_ref)
        vs = pltpu.roll(v, 2 * q, axis=1)
        vlo = jnp.where(low, v, vs)
        vhi = jnp.where(low, vs, v)
        for vi in range(RW_HD):
            vrow = (vlo if vi < RW_KH else vhi)[vi % RW_KH:vi % RW_KH + 1, :]
            s = s_ref[vi]
            p = jnp.sum(s * a, axis=0, keepdims=True)
            sa = p + pltpu.roll(p, 2 * q, axis=1)
            s = s * w + sa * bt + vrow * kd
            s_ref[vi] = s
            p = jnp.sum(s * r, axis=0, keepdims=True)
            y = p + pltpu.roll(p, 2 * q, axis=1)
            yf_ref[pl.ds(i * RW_HD + vi, 1), :] = y
            yb_ref[pl.ds((ts - 1 - i) * RW_HD + vi, 1), :] = y
        return carry

    lax.fori_loop(0, ts, step, 0)


def _from_chains_kernel(yf_ref, yb_ref, of_ref, ob_ref, scr_ref, *, nb, tt):
    q = nb * RW_H
    for d, (y_ref, o_ref) in enumerate(((yf_ref, of_ref), (yb_ref, ob_ref))):
        for vi in range(RW_HD):
            st = y_ref[pl.ds(vi, tt, stride=RW_HD), :].T
            scr_ref[pl.ds(vi, q, stride=RW_HD), :] = st[d * q:(d + 1) * q, :]
        for b in range(nb):
            o_ref[b] = scr_ref[b * RW_W:(b + 1) * RW_W, :].T


def _rw_scan(r, v, a, w, b, kd, ts, tt, ctx_len):
    nb, nt, _ = r.shape
    lanes = 4 * nb * RW_H
    one = lambda x: x.reshape(1, nb, nt, RW_W)
    r1, v1, a1 = one(r), one(v), one(a)
    fa = _to_chains(w, 0, b, 0, tt)
    fb = _to_chains(kd, 0, a1, 0, tt)
    fc = _to_chains(r1, 0, v1, 0, tt)
    ba = _to_chains(w, 1, b, 1, tt)
    bb = _to_chains(kd, 1, a1, 0, tt)
    n_ctx = ctx_len // ts
    n_all = nt // ts
    fwd = lambda g: (g, 0)
    bwd = lambda g: (jnp.where(g < n_ctx, n_ctx - 1 - g, n_all - 1 - g + n_ctx), 0)
    kin = lambda m: pl.BlockSpec((ts * RW_KH, lanes), m)
    yout = lambda m: pl.BlockSpec((ts * RW_HD, lanes), m)
    yshape = jax.ShapeDtypeStruct((nt * RW_HD, lanes), F32)
    yf, yb = pl.pallas_call(
        functools.partial(_rw_scan_kernel, ts=ts, lanes=lanes),
        grid=(n_all,),
        in_specs=[kin(fwd), kin(fwd), kin(fwd), kin(bwd), kin(bwd), kin(bwd)],
        out_specs=[yout(fwd), yout(bwd)],
        out_shape=[yshape, yshape],
        scratch_shapes=[pltpu.VMEM((RW_HD, RW_KH, lanes), F32)],
        compiler_params=_params(("arbitrary",)),
    )(fa, fb, fc, ba, bb, fc)
    nat = jax.ShapeDtypeStruct((nb, nt, RW_W), F32)
    return pl.pallas_call(
        functools.partial(_from_chains_kernel, nb=nb, tt=tt),
        grid=(nt // tt,),
        in_specs=[pl.BlockSpec((tt * RW_HD, lanes), lambda i: (i, 0))] * 2,
        out_specs=[pl.BlockSpec((nb, tt, RW_W), lambda i: (0, i, 0))] * 2,
        out_shape=[nat, nat],
        scratch_shapes=[pltpu.VMEM((nb * RW_W, tt), F32)],
        compiler_params=_params(("parallel",)),
    )(yf, yb)


def _outproj_kernel(x_ref, mod_ref, ys_ref, at_ref, wf_ref, wb_ref, bon_ref, g_ref,
                    gw_ref, gb_ref, sg_ref, lg_ref, lb_ref, avg_ref, wo_ref, o_ref, *, tm, ctx_len):
    j = pl.program_id(1)
    row = j * tm + lax.broadcasted_iota(jnp.int32, (tm, 1), 0)
    is_ctx = row < ctx_len
    a = jax.nn.gelu(ys_ref[0])
    o1 = _rms(a * _sigmoid(_dot(a, gw_ref[...], HI) + gb_ref[...])) * sg_ref[...]
    wkv = wf_ref[0] + wb_ref[0]
    avg = avg_ref[...]
    cen = wkv - _dot(wkv, avg, HI)
    var = _dot(cen * cen, avg, HI)
    o3 = (cen * lax.rsqrt(var + RW_LN_EPS) * lg_ref[...] + lb_ref[...] + bon_ref[0]) * g_ref[0]
    o = (_dot(o1.astype(BF16), wo_ref[0:S5_W, :])
         + _dot(at_ref[0].astype(BF16), wo_ref[S5_W:S5_W + ATT_W, :])
         + _dot(o3.astype(BF16), wo_ref[S5_W + ATT_W:, :]))
    o_ref[0] = x_ref[0] + _pick_mod(mod_ref[0], is_ctx, 2) * o


def _outproj(x, mod, ys5, att, wkv_f, wkv_b, bonus, g, glu_w, glu_b, s5_out_g, ln_g, ln_b, w_out, tm, ctx_len):
    nb, nt, d = x.shape
    row = lambda a: a.reshape(1, -1)
    avg = _block_diag_mean(RW_W, RW_HD)
    params = (glu_w, row(glu_b), row(s5_out_g), row(ln_g), row(ln_b), avg, w_out.astype(BF16))
    full = lambda a: pl.BlockSpec(a.shape, lambda b, j: (0,) * a.ndim)
    tok = lambda w: pl.BlockSpec((1, tm, w), lambda b, j: (b, j, 0))
    return pl.pallas_call(
        functools.partial(_outproj_kernel, tm=tm, ctx_len=ctx_len),
        grid=(nb, nt // tm),
        in_specs=[tok(d), pl.BlockSpec((1, 2, N_MOD * d), lambda b, j: (b, 0, 0)),
                  tok(S5_W), tok(ATT_W), tok(RW_W), tok(RW_W), tok(RW_W), tok(RW_W)] + [full(a) for a in params],
        out_specs=tok(d),
        out_shape=jax.ShapeDtypeStruct(x.shape, F32),
        compiler_params=_params(("parallel", "parallel")),
    )(x, mod, ys5, att, wkv_f, wkv_b, bonus, g, *params)


FFN_HALO = 16


def _ffn_kernel(x_ref, xp_ref, xn_ref, mod_ref, g_ref, up_ref, cw_ref, cb_ref, dn_ref, o_ref,
                hs_ref, acc_ref, *, tm, tf, ctx_len, nt):
    j = pl.program_id(1)
    f = pl.program_id(2)
    hl = FFN_HALO
    mod = mod_ref[0]

    def adaln(xv, first_row):
        rows = first_row + lax.broadcasted_iota(jnp.int32, (xv.shape[0], 1), 0)
        is_ctx = rows < ctx_len
        h = _rms(xv) * g_ref[...]
        return (h * (1.0 + _pick_mod(mod, is_ctx, 4)) + _pick_mod(mod, is_ctx, 3)).astype(BF16)

    @pl.when(f == 0)
    def _():
        hs_ref[0:hl, :] = adaln(xp_ref[0], j * tm - hl)
        hs_ref[hl:hl + tm, :] = adaln(x_ref[0], j * tm)
        hs_ref[hl + tm:, :] = adaln(xn_ref[0], (j + 1) * tm)
        acc_ref[...] = jnp.zeros_like(acc_ref)

    u = _dot(hs_ref[...], up_ref[0])
    n_ext = tm + 2 * hl
    row = j * tm + lax.broadcasted_iota(jnp.int32, (tm, 1), 0)
    cen = u[hl:hl + tm]
    prev = pltpu.roll(u, 1, axis=0)[hl:hl + tm]
    prev = jnp.where((row == 0) | (row == ctx_len), 0.0, prev)
    nxt = pltpu.roll(u, n_ext - 1, axis=0)[hl:hl + tm]
    nxt = jnp.where((row == ctx_len - 1) | (row == nt - 1), 0.0, nxt)
    cw = cw_ref[0]
    c = cw[0:1] * prev + cw[1:2] * cen + cw[2:3] * nxt + cb_ref[0]
    gate = c[:, :tf]
    act = gate * _sigmoid(gate) * c[:, tf:]
    acc_ref[...] += _dot(act.astype(BF16), dn_ref[...])

    @pl.when(f == pl.num_programs(2) - 1)
    def _():
        rows = j * tm + lax.broadcasted_iota(jnp.int32, (tm, 1), 0)
        o_ref[0] = x_ref[0] + _pick_mod(mod, rows < ctx_len, 5) * acc_ref[...]


def _conv_ffn(x, mod, g, up, conv_w, conv_b, down, tm, tf, ctx_len):
    nb, nt, d = x.shape
    nf = D_FF // tf
    hl = FFN_HALO
    upb = up.astype(BF16)
    up_t = jnp.concatenate([upb[:, :D_FF].reshape(d, nf, tf), upb[:, D_FF:].reshape(d, nf, tf)], axis=2)
    up_t = up_t.transpose(1, 0, 2)
    pair = lambda a: jnp.concatenate([a[..., :D_FF].reshape(-1, nf, tf), a[..., D_FF:].reshape(-1, nf, tf)],
                                     axis=2).transpose(1, 0, 2)
    cw_t = pair(conv_w)
    cb_t = pair(conv_b.reshape(1, -1))
    nh = nt // hl
    return pl.pallas_call(
        functools.partial(_ffn_kernel, tm=tm, tf=tf, ctx_len=ctx_len, nt=nt),
        grid=(nb, nt // tm, nf),
        in_specs=[pl.BlockSpec((1, tm, d), lambda b, j, f: (b, j, 0)),
                  pl.BlockSpec((1, hl, d), lambda b, j, f: (b, jnp.maximum(j * (tm // hl) - 1, 0), 0)),
                  pl.BlockSpec((1, hl, d), lambda b, j, f: (b, jnp.minimum((j + 1) * (tm // hl), nh - 1), 0)),
                  pl.BlockSpec((1, 2, N_MOD * d), lambda b, j, f: (b, 0, 0)),
                  pl.BlockSpec((1, d), lambda b, j, f: (0, 0)),
                  pl.BlockSpec((1, d, 2 * tf), lambda b, j, f: (f, 0, 0)),
                  pl.BlockSpec((1, 3, 2 * tf), lambda b, j, f: (f, 0, 0)),
                  pl.BlockSpec((1, 1, 2 * tf), lambda b, j, f: (f, 0, 0)),
                  pl.BlockSpec((tf, d), lambda b, j, f: (f, 0))],
        out_specs=pl.BlockSpec((1, tm, d), lambda b, j, f: (b, j, 0)),
        out_shape=jax.ShapeDtypeStruct(x.shape, F32),
        scratch_shapes=[pltpu.VMEM((tm + 2 * hl, d), BF16), pltpu.VMEM((tm, d), F32)],
        compiler_params=_params(("parallel", "parallel", "arbitrary")),
    )(x, x, x, mod, g.reshape(1, d), up_t, cw_t, cb_t, down.astype(BF16))


def _final_kernel(x_ref, g_ref, o_ref):
    o_ref[0] = _rms(x_ref[0]) * g_ref[...]


def _final_norm(x, g, ctx_len, tm):
    nb, nt, d = x.shape
    off = ctx_len // tm
    return pl.pallas_call(
        _final_kernel,
        grid=(nb, (nt - ctx_len) // tm),
        in_specs=[pl.BlockSpec((1, tm, d), lambda b, j: (b, j + off, 0)),
                  pl.BlockSpec((1, d), lambda b, j: (0, 0))],
        out_specs=pl.BlockSpec((1, tm, d), lambda b, j: (b, j, 0)),
        out_shape=jax.ShapeDtypeStruct((nb, nt - ctx_len, d), F32),
        compiler_params=_params(("parallel", "parallel")),
    )(x, g.reshape(1, d))


def _forward(p, *, tm, tq, ts, tt, tf, tfin):
    x, ctx = p["x"], p["ctx"]
    ctx_len = ctx.shape[1]
    depth = p["mod_w"].shape[0]
    xa = jnp.concatenate([ctx, x], axis=1)
    mods = _modulations(p["c"], p["c_ctx"], p["mod_w"], p["mod_b"])
    for l in range(depth):
        mod = mods[l]
        zs5, zq, zkv, zrw = _inproj(xa, mod, p["norm1_g"][l].reshape(1, -1), p["w_in"][l], tm, ctx_len)
        ops = _s5_operators(p["s5_a_re"][l], p["s5_a_im"][l], p["s5_log_step"][l], p["s5_b_re"][l],
                            p["s5_b_im"][l], p["s5_c_re"][l], p["s5_c_im"][l], p["s5_d"][l])
        ys5 = _s5_scan(zs5, ops, ctx_len)
        att = _attention(zq, zkv, p["att_qn_g"][l], p["att_kn_g"][l], p["att_out_g"][l], tq, ctx_len)
        r, v, a, w, b, kd, g, bonus = _rw_prep(zrw, p["rw_mu"][l], p["rw_w0"][l], p["rw_w2"][l], p["rw_a0"][l],
                                                 p["rw_a2"][l], p["rw_g2"][l], p["rw_k_k"][l], p["rw_k_a"][l],
                                                 p["rw_r_k"][l], tm, ctx_len)
        wkv_f, wkv_b = _rw_scan(r, v, a, w, b, kd, ts, tt, ctx_len)
        xa = _outproj(xa, mod, ys5, att, wkv_f, wkv_b, bonus, g, p["s5_glu_w"][l], p["s5_glu_b"][l],
                      p["s5_out_g"][l], p["rw_ln_g"][l], p["rw_ln_b"][l], p["w_out"][l], tm, ctx_len)
        xa = _conv_ffn(xa, mod, p["norm2_g"][l], p["ffn_up"][l], p["ffn_conv_w"][l], p["ffn_conv_b"][l],
                       p["ffn_down"][l], tm, tf, ctx_len)
    return _final_norm(xa, p["final_g"], ctx_len, tfin)


_ARG_NAMES = ("x c ctx c_ctx norm1_g norm2_g mod_w mod_b w_in w_out s5_a_re s5_a_im s5_log_step s5_b_re s5_b_im "
              "s5_c_re s5_c_im s5_d s5_glu_w s5_glu_b s5_out_g att_qn_g att_kn_g att_out_g rw_mu rw_w0 rw_w2 "
              "rw_a0 rw_a2 rw_g2 rw_k_k rw_k_a rw_r_k rw_ln_g rw_ln_b ffn_up ffn_conv_w ffn_conv_b ffn_down "
              "final_g").split()


def kernel(x, c, ctx, c_ctx, norm1_g, norm2_g, mod_w, mod_b, w_in, w_out, s5_a_re, s5_a_im, s5_log_step, s5_b_re, s5_b_im, s5_c_re, s5_c_im, s5_d, s5_glu_w, s5_glu_b, s5_out_g, att_qn_g, att_kn_g, att_out_g, rw_mu, rw_w0, rw_w2, rw_a0, rw_a2, rw_g2, rw_k_k, rw_k_a, rw_r_k, rw_ln_g, rw_ln_b, ffn_up, ffn_conv_w, ffn_conv_b, ffn_down, final_g):
    args = (x, c, ctx, c_ctx, norm1_g, norm2_g, mod_w, mod_b, w_in, w_out, s5_a_re, s5_a_im, s5_log_step, s5_b_re,
            s5_b_im, s5_c_re, s5_c_im, s5_d, s5_glu_w, s5_glu_b, s5_out_g, att_qn_g, att_kn_g, att_out_g, rw_mu,
            rw_w0, rw_w2, rw_a0, rw_a2, rw_g2, rw_k_k, rw_k_a, rw_r_k, rw_ln_g, rw_ln_b, ffn_up, ffn_conv_w,
            ffn_conv_b, ffn_down, final_g)
    return _forward(dict(zip(_ARG_NAMES, args)), tm=768, tq=256, ts=32, tt=128, tf=256, tfin=256)
```

```python
import functools
import math

import jax
import jax.numpy as jnp
from jax import lax
from jax.experimental import pallas as pl
from jax.experimental.pallas import tpu as pltpu

F32 = jnp.float32
BF16 = jnp.bfloat16
HI = lax.Precision.HIGHEST

D_MODEL = 1024
GRID_W = 64
S5_W = 256
S5_GH = 16
S5_G = 16
S5_P = 64
HEAD_DIM = 64
ATT_W = 512
N_Q = 8
GQA_REP = 4
N_KV = 2
ATT_KV = 128
ROPE_PAIRS = 16
ROPE_THETA = 10000.0
RW_W = 256
RW_HD = 64
RW_H = 4
LORA_W = 32
LORA_A = 32
LORA_G = 64
RW_IN = 960
RW_PAD = 1024
RW_LN_EPS = 64e-5
D_FF = 2816
N_MOD = 6
RMS_EPS = 1e-6

S5_T = 32
V7X_VMEM_LIMIT = 56 * 1024 * 1024


def _params(sem, vmem=V7X_VMEM_LIMIT):
    return pltpu.CompilerParams(dimension_semantics=sem, vmem_limit_bytes=vmem)


def _rms(x):
    return x * lax.rsqrt(jnp.mean(x * x, axis=-1, keepdims=True) + RMS_EPS)


def _sigmoid(x):
    return 1.0 / (1.0 + jnp.exp(-x))


def _dot(a, b, precision=None):
    return jnp.dot(a, b, precision=precision, preferred_element_type=F32)


def _split2(x):
    hi = x.astype(BF16)
    return hi, (x - hi.astype(F32)).astype(BF16)


def _dot_x3(x, w):
    xh, xl = _split2(x)
    wh, wl = _split2(w)
    return _dot(jnp.concatenate([xh, xl, xh], axis=1), jnp.concatenate([wh, wh, wl], axis=0))


def _seg_dot(x, m):
    xh, xl = _split2(x)
    mb = m.astype(BF16)
    return _dot(jnp.concatenate([xh, xl], axis=1), jnp.concatenate([mb, mb], axis=0))


def _block_diag_mean(width, seg):
    i = jnp.arange(width)
    return ((i[:, None] // seg) == (i[None, :] // seg)).astype(F32) / seg


def _mod_kernel(c_ref, w_ref, b_ref, o_ref):
    c = c_ref[...]
    s = c * _sigmoid(c)
    o_ref[0] = _dot(s, w_ref[0], HI) + b_ref[0]


def _modulations(c, c_ctx, mod_w, mod_b):
    depth, d, n = mod_w.shape
    nb = c.shape[0]
    rows = ((nb + 1 + 7) // 8) * 8
    cc = jnp.zeros((rows, d), F32).at[:nb].set(c).at[nb].set(c_ctx)
    tn = n // 4
    out = pl.pallas_call(
        _mod_kernel,
        grid=(depth, n // tn),
        in_specs=[pl.BlockSpec((rows, d), lambda l, j: (0, 0)),
                  pl.BlockSpec((1, d, tn), lambda l, j: (l, 0, j)),
                  pl.BlockSpec((1, 1, tn), lambda l, j: (l, 0, j))],
        out_specs=pl.BlockSpec((1, rows, tn), lambda l, j: (l, 0, j)),
        out_shape=jax.ShapeDtypeStruct((depth, rows, n), F32),
        compiler_params=_params(("parallel", "parallel")),
    )(cc, mod_w, mod_b.reshape(depth, 1, n))
    lat = out[:, :nb]
    ctx = jnp.broadcast_to(out[:, nb:nb + 1], lat.shape)
    return jnp.stack([lat, ctx], axis=2)


def _pick_mod(mod, is_ctx, idx):
    lo = idx * D_MODEL
    return jnp.where(is_ctx, mod[1:2, lo:lo + D_MODEL], mod[0:1, lo:lo + D_MODEL])


def _inproj_kernel(x_ref, mod_ref, g_ref, ws_ref, wq_ref, wkv_ref, wr_ref,
                   s5_ref, q_ref, kv_ref, rw_ref, *, tm, ctx_len):
    j = pl.program_id(1)
    x = x_ref[0]
    row = j * tm + lax.broadcasted_iota(jnp.int32, (tm, 1), 0)
    is_ctx = row < ctx_len
    mod = mod_ref[0]
    h = _rms(x) * g_ref[...]
    h = h * (1.0 + _pick_mod(mod, is_ctx, 1)) + _pick_mod(mod, is_ctx, 0)
    hb = h.astype(BF16)
    s5_ref[0] = _dot(hb, ws_ref[...])
    q_ref[0] = _dot(hb, wq_ref[...])
    kv_ref[0] = _dot(hb, wkv_ref[...])
    rw_ref[0] = _dot(hb, wr_ref[...])


def _inproj(x, mod, g, w_in, tm, ctx_len):
    nb, nt, d = x.shape
    wb = w_in.astype(BF16)
    ws = wb[:, :S5_W]
    wq = wb[:, S5_W:S5_W + ATT_W]
    wkv = wb[:, S5_W + ATT_W:S5_W + ATT_W + 2 * ATT_KV]
    wr = jnp.pad(wb[:, S5_W + ATT_W + 2 * ATT_KV:], ((0, 0), (0, RW_PAD - RW_IN)))
    full = lambda a: pl.BlockSpec(a.shape, lambda b, j: (0,) * a.ndim)
    tok = lambda w: pl.BlockSpec((1, tm, w), lambda b, j: (b, j, 0))
    return pl.pallas_call(
        functools.partial(_inproj_kernel, tm=tm, ctx_len=ctx_len),
        grid=(nb, nt // tm),
        in_specs=[tok(d), pl.BlockSpec((1, 2, N_MOD * d), lambda b, j: (b, 0, 0)),
                  full(g), full(ws), full(wq), full(wkv), full(wr)],
        out_specs=[tok(S5_W), tok(ATT_W), tok(2 * ATT_KV), tok(RW_PAD)],
        out_shape=[jax.ShapeDtypeStruct((nb, nt, w), F32) for w in (S5_W, ATT_W, 2 * ATT_KV, RW_PAD)],
        compiler_params=_params(("parallel", "parallel")),
    )(x, mod, g, ws, wq, wkv, wr)


def _s5_operators(a_re, a_im, log_step, b_re, b_im, c_re, c_im, d):
    T = S5_T
    lam = lax.complex(a_re.astype(F32), a_im.astype(F32))
    step = jnp.exp(log_step.astype(F32))[..., None]
    lam_bar = jnp.exp(lam * step)
    bmat = lax.complex(b_re.astype(F32), b_im.astype(F32))
    b_bar = ((lam_bar - 1.0) / lam)[..., None] * bmat
    cmat = lax.complex(c_re.astype(F32), c_im.astype(F32))
    n = jnp.arange(T + 1, dtype=F32)
    pw = jnp.exp((lam * step)[..., None] * n)
    cp = jnp.swapaxes(cmat, -1, -2)[:, :, :, None, :] * pw[..., None]
    bp = jnp.transpose(pw, (0, 1, 3, 2))[:, :, :, None, :] * jnp.transpose(b_bar, (0, 1, 3, 2))[:, :, None]
    G, P, GH = S5_G, S5_P, S5_GH
    W = T * GH

    def ri_rows(z):
        return jnp.concatenate([jnp.real(z), -jnp.imag(z)], axis=1)

    def ri_cols(z):
        return jnp.concatenate([jnp.real(z), jnp.imag(z)], axis=2)

    bt = jnp.stack([jnp.concatenate([jnp.real(jnp.swapaxes(b_bar[di], -1, -2)),
                                     jnp.imag(jnp.swapaxes(b_bar[di], -1, -2))], axis=-1)
                    for di in range(2)], axis=1)
    cpk0 = ri_rows(cp[0, :, :, :T].reshape(G, P, W))
    cpk1 = ri_rows(cp[1, :, :, :T][:, :, ::-1].reshape(G, P, W))
    cc = jnp.concatenate([ri_rows(cp[0, :, :, 1:].reshape(G, P, W)),
                          ri_rows(cp[1, :, :, 1:][:, :, ::-1].reshape(G, P, W))], axis=1)
    bc = jnp.concatenate([ri_cols(bp[0, :, :T][:, ::-1].reshape(G, W, P)),
                          ri_cols(bp[1, :, :T].reshape(G, W, P))], axis=2)
    lt = pw[..., T]
    lam_rows = jnp.stack([jnp.concatenate([jnp.real(lt[0]), jnp.real(lt[0])], -1),
                          jnp.concatenate([-jnp.imag(lt[0]), jnp.imag(lt[0])], -1),
                          jnp.concatenate([jnp.real(lt[1]), jnp.real(lt[1])], -1),
                          jnp.concatenate([-jnp.imag(lt[1]), jnp.imag(lt[1])], -1)], axis=1)
    dvec = jnp.tile(d.astype(F32).reshape(G, 1, GH), (1, 1, T))
    return bt, cpk0, cpk1, bc, cc, lam_rows, dvec


def _s5_kernel(u_ref, bt_ref, cp0_ref, cp1_ref, bc_ref, cc_ref, lam_ref, d_ref, y_ref,
               kt_ref, sp_ref, *, n_ctx, n_chunks, nb):
    T, GH, P2 = S5_T, S5_GH, 2 * S5_P
    W = T * GH
    krow0 = _dot(bt_ref[0, 0], cp0_ref[0], HI)
    krow1 = _dot(bt_ref[0, 1], cp1_ref[0], HI)
    lane = lax.broadcasted_iota(jnp.int32, (GH, W), 1)
    for s in range(T):
        f = krow0 if s == 0 else pltpu.roll(krow0, s * GH, axis=1)
        f = jnp.where(lane >= s * GH, f, 0.0)
        sh = (T - 1 - s) * GH
        b = krow1 if sh == 0 else pltpu.roll(krow1, W - sh, axis=1)
        b = jnp.where(lane < (s + 1) * GH, b, 0.0)
        kt_ref[s * GH:(s + 1) * GH, :] = f + b
    u = u_ref[0]
    y = u * d_ref[0] + _dot_x3(u, kt_ref[...])
    e = _dot_x3(u, bc_ref[0])
    lam = lam_ref[0]

    def run(order, col, ra, rb):
        s = jnp.zeros((nb, P2), F32)
        for c in order:
            rows = slice(c * nb, (c + 1) * nb)
            sp_ref[rows, col:col + P2] = s
            s = ra * s + rb * pltpu.roll(s, S5_P, axis=1) + e[rows, col:col + P2]

    run(list(range(n_chunks)), 0, lam[0:1], lam[1:2])
    run(list(range(n_ctx - 1, -1, -1)) + list(range(n_chunks - 1, n_ctx - 1, -1)), P2, lam[2:3], lam[3:4])
    y_ref[0] = y + _dot_x3(sp_ref[...], cc_ref[0])


def _s5_scan(zs5, ops, ctx_len):
    nb, nt, _ = zs5.shape
    T, G, GH = S5_T, S5_G, S5_GH
    W = T * GH
    n_chunks = nt // T
    rows = n_chunks * nb
    u = zs5.reshape(nb, n_chunks, T, G, GH).transpose(3, 1, 0, 2, 4).reshape(G, rows, W)
    bt, cpk0, cpk1, bc, cc, lam_rows, dvec = ops
    per_g = lambda a: pl.BlockSpec((1,) + a.shape[1:], lambda g: (g,) + (0,) * (a.ndim - 1))
    y = pl.pallas_call(
        functools.partial(_s5_kernel, n_ctx=ctx_len // T, n_chunks=n_chunks, nb=nb),
        grid=(G,),
        in_specs=[per_g(u), per_g(bt), per_g(cpk0), per_g(cpk1), per_g(bc), per_g(cc),
                  per_g(lam_rows), per_g(dvec)],
        out_specs=per_g(u),
        out_shape=jax.ShapeDtypeStruct(u.shape, F32),
        scratch_shapes=[pltpu.VMEM((W, W), F32), pltpu.VMEM((rows, 4 * S5_P), F32)],
        compiler_params=_params(("parallel",)),
    )(u, bt, cpk0, cpk1, bc, cc, lam_rows, dvec)
    return y.reshape(G, n_chunks, nb, T, GH).transpose(2, 1, 3, 0, 4).reshape(nb, nt, S5_W)


def _rope_tables(nt, ctx_len, width):
    tok = jnp.arange(nt - ctx_len, dtype=jnp.int32)
    pos_row = (tok // GRID_W).astype(F32)
    pos_col = (tok % GRID_W).astype(F32)
    inv = ROPE_THETA ** (-jnp.arange(ROPE_PAIRS, dtype=F32) / ROPE_PAIRS)
    ang_r = pos_row[:, None] * inv
    ang_c = pos_col[:, None] * inv
    ang = jnp.concatenate([ang_r, ang_r, ang_c, ang_c], axis=1)
    sign = jnp.tile(jnp.concatenate([-jnp.ones(ROPE_PAIRS, F32), jnp.ones(ROPE_PAIRS, F32)]), 2)
    cos = jnp.concatenate([jnp.ones((ctx_len, HEAD_DIM), F32), jnp.cos(ang)], axis=0)
    sin = jnp.concatenate([jnp.zeros((ctx_len, HEAD_DIM), F32), jnp.sin(ang) * sign], axis=0)
    reps = width // HEAD_DIM
    return jnp.tile(cos, (1, reps)), jnp.tile(sin, (1, reps))


def _head_norm_rope(t, pmat, g, cos, sin):
    width = t.shape[-1]
    ms = _seg_dot(t * t, pmat)
    tn = t * lax.rsqrt(ms + RMS_EPS) * g
    lane = lax.broadcasted_iota(jnp.int32, tn.shape, 1)
    first = (lane % (2 * ROPE_PAIRS)) < ROPE_PAIRS
    partner = jnp.where(first, pltpu.roll(tn, width - ROPE_PAIRS, axis=1), pltpu.roll(tn, ROPE_PAIRS, axis=1))
    return tn * cos + partner * sin


def _attn_kernel(q_ref, kv_ref, cq_ref, sq_ref, ck_ref, sk_ref, qg_ref, kg_ref, og_ref, pq_ref, pk_ref,
                 o_ref, kn_ref, vb_ref, *, tq, ctx_len, nk):
    j = pl.program_id(1)

    @pl.when(j == 0)
    def _():
        kv = kv_ref[0]
        kr = _head_norm_rope(kv[:, :ATT_KV], pk_ref[...], kg_ref[...], ck_ref[...], sk_ref[...])
        ones = jnp.ones((kv.shape[0], HEAD_DIM), BF16)
        for g in range(N_KV):
            kn_ref[g] = kr[:, g * HEAD_DIM:(g + 1) * HEAD_DIM].astype(BF16)
            vb_ref[g, :, :HEAD_DIM] = kv[:, ATT_KV + g * HEAD_DIM:ATT_KV + (g + 1) * HEAD_DIM].astype(BF16)
            vb_ref[g, :, HEAD_DIM:] = ones

    qr = _head_norm_rope(q_ref[0], pq_ref[...], qg_ref[...], cq_ref[...], sq_ref[...]) * (
        HEAD_DIM ** -0.5 * math.log2(math.e))

    def attend(n_keys):
        outs = []
        for h in range(N_Q):
            g = h // GQA_REP
            qh = qr[:, h * HEAD_DIM:(h + 1) * HEAD_DIM].astype(BF16)
            s = lax.dot_general(qh, kn_ref[g, :n_keys, :], (((1,), (1,)), ((), ())),
                                preferred_element_type=F32)
            p = jnp.exp2(s - jnp.max(s, axis=-1, keepdims=True))
            ov = _dot(p.astype(BF16), vb_ref[g, :n_keys, :])
            outs.append(ov[:, :HEAD_DIM] / ov[:, HEAD_DIM:])
        o = jnp.concatenate(outs, axis=1)
        o_ref[0] = _rms(o) * og_ref[...]

    @pl.when(j * tq < ctx_len)
    def _():
        attend(ctx_len)

    @pl.when(j * tq >= ctx_len)
    def _():
        attend(nk)


def _attention(zq, zkv, qn_g, kn_g, out_g, tq, ctx_len):
    nb, nt, _ = zq.shape
    cq, sq = _rope_tables(nt, ctx_len, ATT_W)
    ck, sk = cq[:, :ATT_KV], sq[:, :ATT_KV]
    qg = jnp.tile(qn_g, N_Q).reshape(1, ATT_W)
    kg = jnp.tile(kn_g, N_KV).reshape(1, ATT_KV)
    pq = _block_diag_mean(ATT_W, HEAD_DIM)
    pk = _block_diag_mean(ATT_KV, HEAD_DIM)
    full = lambda a: pl.BlockSpec(a.shape, lambda b, j: (0,) * a.ndim)
    return pl.pallas_call(
        functools.partial(_attn_kernel, tq=tq, ctx_len=ctx_len, nk=nt),
        grid=(nb, nt // tq),
        in_specs=[pl.BlockSpec((1, tq, ATT_W), lambda b, j: (b, j, 0)),
                  pl.BlockSpec((1, nt, 2 * ATT_KV), lambda b, j: (b, 0, 0)),
                  pl.BlockSpec((tq, ATT_W), lambda b, j: (j, 0)),
                  pl.BlockSpec((tq, ATT_W), lambda b, j: (j, 0)),
                  full(ck), full(sk), full(qg), full(kg),
                  pl.BlockSpec((1, ATT_W), lambda b, j: (0, 0)), full(pq), full(pk)],
        out_specs=pl.BlockSpec((1, tq, ATT_W), lambda b, j: (b, j, 0)),
        out_shape=jax.ShapeDtypeStruct((nb, nt, ATT_W), F32),
        scratch_shapes=[pltpu.VMEM((N_KV, nt, HEAD_DIM), BF16), pltpu.VMEM((N_KV, nt, 2 * HEAD_DIM), BF16)],
        compiler_params=_params(("parallel", "arbitrary")),
    )(zq, zkv, cq, sq, ck, sk, qg, kg, out_g.reshape(1, ATT_W), pq, pk)


def _rw_prep_kernel(z_ref, zp_ref, zn_ref, mu_ref, wl_ref, w0_ref, a0_ref, kk_ref, ka_ref, rk_ref, ones_ref,
                    r_ref, v_ref, a_ref, w_ref, b_ref, kd_ref, g_ref, bon_ref, *, tm, ctx_len, nt):
    j = pl.program_id(1)
    z = z_ref[0]
    row_l = lax.broadcasted_iota(jnp.int32, (tm, 1), 0)
    row = j * tm + row_l
    prev = jnp.where(row_l == 0, zp_ref[0, 7:8, :], pltpu.roll(z, 1, axis=0))
    prev = jnp.where((row == 0) | (row == ctx_len), 0.0, prev)
    nxt = jnp.where(row_l == tm - 1, zn_ref[0, 0:1, :], pltpu.roll(z, tm - 1, axis=0))
    nxt = jnp.where((row == ctx_len - 1) | (row == nt - 1), 0.0, nxt)
    zs = z + mu_ref[...] * (0.5 * (prev + nxt) - z)
    r = zs[:, 0:RW_W]
    k = zs[:, RW_W:2 * RW_W]
    v = zs[:, 2 * RW_W:3 * RW_W]
    lo = zs[:, 3 * RW_W:]
    lane = lax.broadcasted_iota(jnp.int32, lo.shape, 1)
    lo = jnp.where(lane < 2 * LORA_W, jnp.tanh(lo),
                   jnp.where(lane < 2 * LORA_W + 2 * LORA_A, lo, _sigmoid(lo)))
    proj = _dot_x3(lo, wl_ref[...])
    ones = ones_ref[...]
    kk = k * kk_ref[...]
    nrm = jnp.sqrt(_seg_dot(kk * kk, ones))
    kk = kk / jnp.maximum(nrm, 1e-12)
    r_ref[0] = r
    v_ref[0] = v
    a_ref[0] = -kk
    g_ref[0] = proj[:, 4 * RW_W:5 * RW_W]
    bon = jnp.zeros_like(r)
    for di in range(2):
        pre = w0_ref[di:di + 1, :] + proj[:, di * RW_W:(di + 1) * RW_W]
        w_log = -(jnp.maximum(-pre, 0.0) + jnp.log(1.0 + jnp.exp(-jnp.abs(pre)))) - 0.5
        w_ref[di, 0] = jnp.exp(-jnp.exp(w_log))
        iclr = _sigmoid(a0_ref[di:di + 1, :] + proj[:, (2 + di) * RW_W:(3 + di) * RW_W])
        kd = k * (1.0 + (iclr - 1.0) * ka_ref[...])
        kd_ref[di, 0] = kd
        b_ref[di, 0] = kk * iclr
        bon = bon + _seg_dot(r * kd * rk_ref[...], ones) * v
    bon_ref[0] = bon


def _rw_prep(zrw, mu, w0, w2, a0, a2, g2, k_k, k_a, r_k, tm, ctx_len):
    nb, nt, _ = zrw.shape
    wl = jnp.zeros((RW_PAD - 3 * RW_W, 5 * RW_W), F32)
    wl = wl.at[0:LORA_W, 0:RW_W].set(w2[0]).at[LORA_W:2 * LORA_W, RW_W:2 * RW_W].set(w2[1])
    o = 2 * LORA_W
    wl = wl.at[o:o + LORA_A, 2 * RW_W:3 * RW_W].set(a2[0]).at[o + LORA_A:o + 2 * LORA_A, 3 * RW_W:4 * RW_W].set(a2[1])
    o = 2 * LORA_W + 2 * LORA_A
    wl = wl.at[o:o + LORA_G, 4 * RW_W:5 * RW_W].set(g2)
    mu_p = jnp.pad(mu, (0, RW_PAD - RW_IN)).reshape(1, RW_PAD)
    ones = _block_diag_mean(RW_W, RW_HD) * RW_HD
    row = lambda a: a.reshape(1, RW_W)
    full = lambda a: pl.BlockSpec(a.shape, lambda b, j: (0,) * a.ndim)
    tok = pl.BlockSpec((1, tm, RW_W), lambda b, j: (b, j, 0))
    tok2 = pl.BlockSpec((2, 1, tm, RW_W), lambda b, j: (0, b, j, 0))
    one = jax.ShapeDtypeStruct((nb, nt, RW_W), F32)
    two = jax.ShapeDtypeStruct((2, nb, nt, RW_W), F32)
    n8 = nt // 8
    args = (zrw, zrw, zrw, mu_p, wl, w0, a0, row(k_k), row(k_a), row(r_k), ones)
    return pl.pallas_call(
        functools.partial(_rw_prep_kernel, tm=tm, ctx_len=ctx_len, nt=nt),
        grid=(nb, nt // tm),
        in_specs=[pl.BlockSpec((1, tm, RW_PAD), lambda b, j: (b, j, 0)),
                  pl.BlockSpec((1, 8, RW_PAD), lambda b, j: (b, jnp.maximum(j * (tm // 8) - 1, 0), 0)),
                  pl.BlockSpec((1, 8, RW_PAD), lambda b, j: (b, jnp.minimum((j + 1) * (tm // 8), n8 - 1), 0)),
                  ] + [full(a) for a in args[3:]],
        out_specs=[tok, tok, tok, tok2, tok2, tok2, tok, tok],
        out_shape=[one, one, one, two, two, two, one, one],
        compiler_params=_params(("parallel", "parallel")),
    )(*args)


RW_VH = RW_HD // 2
RW_SLOTS = 4


def _to_chains_kernel(x0_ref, x1_ref, x2_ref, x3_ref, o_ref, scr_ref, *, nb, tt):
    for j, ref in enumerate((x0_ref, x1_ref, x2_ref, x3_ref)):
        for b in range(nb):
            scr_ref[(j * nb + b) * RW_W:(j * nb + b + 1) * RW_W, :] = ref[0, b].T
    lanes = RW_SLOTS * nb * RW_H
    for k in range(RW_HD):
        o_ref[pl.ds(k, tt, stride=RW_HD), :] = scr_ref[pl.ds(k, lanes, stride=RW_HD), :].T


def _to_chains(xs, tt):
    _, nb, nt, _ = xs[0][0].shape
    lanes = RW_SLOTS * nb * RW_H
    spec = lambda di: pl.BlockSpec((1, nb, tt, RW_W), lambda i: (di, 0, i, 0))
    return pl.pallas_call(
        functools.partial(_to_chains_kernel, nb=nb, tt=tt),
        grid=(nt // tt,),
        in_specs=[spec(di) for _, di in xs],
        out_specs=pl.BlockSpec((tt * RW_HD, lanes), lambda i: (i, 0)),
        out_shape=jax.ShapeDtypeStruct((nt * RW_HD, lanes), F32),
        scratch_shapes=[pltpu.VMEM((lanes * RW_HD, tt), F32)],
        compiler_params=_params(("parallel",)),
    )(*[x for x, _ in xs])


def _rw_scan_kernel(fa_ref, fc_ref, ba_ref, bc_ref, yf_ref, yb_ref, s_ref, vec_ref, *, ts, lanes):
    @pl.when(pl.program_id(0) == 0)
    def _():
        s_ref[...] = jnp.zeros_like(s_ref)

    lane = lax.broadcasted_iota(jnp.int32, (RW_HD, lanes), 1)
    q = lanes // 4
    even = (lane // q) % 2 == 0
    low = lane < 2 * q
    low_v = lax.broadcasted_iota(jnp.int32, (RW_VH, lanes), 1) < 2 * q

    def pairs(x):
        xs = pltpu.roll(x, 2 * q, axis=1)
        return jnp.where(low, x, xs), jnp.where(low, xs, x)

    def merge(f, b):
        return (jnp.where(even, f, pltpu.roll(b, q, axis=1)),
                jnp.where(even, pltpu.roll(f, lanes - q, axis=1), b))

    def prepare(i, slot):
        fr = pl.ds(pl.multiple_of(i * RW_HD, RW_HD), RW_HD)
        br = pl.ds(pl.multiple_of((ts - 1 - i) * RW_HD, RW_HD), RW_HD)
        f01, f23 = pairs(fa_ref[fr, :])
        b01, b23 = pairs(ba_ref[br, :])
        vec_ref[slot, 0], vec_ref[slot, 1] = merge(f01, b01)
        vec_ref[slot, 2], vec_ref[slot, 3] = merge(f23, b23)
        vec_ref[slot, 4], vk = merge(fc_ref[fr, :], bc_ref[br, :])
        vec_ref[slot, 5, :RW_VH] = jnp.where(low_v, vk[:RW_VH], vk[RW_VH:])

    def update(i, slot):
        for vi in range(RW_VH):
            s = s_ref[vi]
            sa = jnp.sum(s * vec_ref[slot, 3], axis=0, keepdims=True)
            s = s * vec_ref[slot, 0] + sa * vec_ref[slot, 1] + vec_ref[slot, 5, vi:vi + 1, :] * vec_ref[slot, 2]
            s_ref[vi] = s
            y = jnp.sum(s * vec_ref[slot, 4], axis=0, keepdims=True)
            yf_ref[pl.ds(i * RW_VH + vi, 1), :] = y
            yb_ref[pl.ds((ts - 1 - i) * RW_VH + vi, 1), :] = y

    prepare(0, 0)

    def two_steps(h, carry):
        i = 2 * h
        prepare(i + 1, 1)
        update(i, 0)
        prepare(jnp.minimum(i + 2, ts - 1), 0)
        update(i + 1, 1)
        return carry

    lax.fori_loop(0, ts // 2, two_steps, 0)


def _from_chains_kernel(yf_ref, yb_ref, of_ref, ob_ref, scr_ref, *, nb, tt):
    q = nb * RW_H
    for d, (y_ref, o_ref) in enumerate(((yf_ref, of_ref), (yb_ref, ob_ref))):
        for vi in range(RW_VH):
            st = y_ref[pl.ds(vi, tt, stride=RW_VH), :].T
            for vh in range(2):
                src = (vh * 2 + d) * q
                scr_ref[pl.ds(vh * RW_VH + vi, q, stride=RW_HD), :] = st[src:src + q, :]
        for b in range(nb):
            o_ref[b] = scr_ref[b * RW_W:(b + 1) * RW_W, :].T


def _rw_scan(r, v, a, w, b, kd, ts, tt, ctx_len):
    nb, nt, _ = r.shape
    lanes = RW_SLOTS * nb * RW_H
    one = lambda x: x.reshape(1, nb, nt, RW_W)
    r1, v1, a1 = one(r), one(v), one(a)
    fa = _to_chains([(w, 0), (b, 0), (kd, 0), (a1, 0)], tt)
    ba = _to_chains([(w, 1), (b, 1), (kd, 1), (a1, 0)], tt)
    fc = _to_chains([(r1, 0), (v1, 0), (r1, 0), (v1, 0)], tt)
    n_ctx = ctx_len // ts
    n_all = nt // ts
    fwd = lambda g: (g, 0)
    bwd = lambda g: (jnp.where(g < n_ctx, n_ctx - 1 - g, n_all - 1 - g + n_ctx), 0)
    kin = lambda m: pl.BlockSpec((ts * RW_HD, lanes), m)
    yout = lambda m: pl.BlockSpec((ts * RW_VH, lanes), m)
    yshape = jax.ShapeDtypeStruct((nt * RW_VH, lanes), F32)
    yf, yb = pl.pallas_call(
        functools.partial(_rw_scan_kernel, ts=ts, lanes=lanes),
        grid=(n_all,),
        in_specs=[kin(fwd), kin(fwd), kin(bwd), kin(bwd)],
        out_specs=[yout(fwd), yout(bwd)],
        out_shape=[yshape, yshape],
        scratch_shapes=[pltpu.VMEM((RW_VH, RW_HD, lanes), F32), pltpu.VMEM((2, 6, RW_HD, lanes), F32)],
        compiler_params=_params(("arbitrary",)),
    )(fa, fc, ba, fc)
    nat = jax.ShapeDtypeStruct((nb, nt, RW_W), F32)
    return pl.pallas_call(
        functools.partial(_from_chains_kernel, nb=nb, tt=tt),
        grid=(nt // tt,),
        in_specs=[pl.BlockSpec((tt * RW_VH, lanes), lambda i: (i, 0))] * 2,
        out_specs=[pl.BlockSpec((nb, tt, RW_W), lambda i: (0, i, 0))] * 2,
        out_shape=[nat, nat],
        scratch_shapes=[pltpu.VMEM((nb * RW_W, tt), F32)],
        compiler_params=_params(("parallel",)),
    )(yf, yb)


def _outproj_kernel(x_ref, mod_ref, ys_ref, at_ref, wf_ref, wb_ref, bon_ref, g_ref,
                    gw_ref, gb_ref, sg_ref, lg_ref, lb_ref, avg_ref, wo_ref, o_ref, *, tm, ctx_len):
    j = pl.program_id(1)
    row = j * tm + lax.broadcasted_iota(jnp.int32, (tm, 1), 0)
    is_ctx = row < ctx_len
    a = jax.nn.gelu(ys_ref[0])
    o1 = _rms(a * _sigmoid(_dot_x3(a, gw_ref[...]) + gb_ref[...])) * sg_ref[...]
    wkv = wf_ref[0] + wb_ref[0]
    avg = avg_ref[...]
    cen = wkv - _seg_dot(wkv, avg)
    var = _seg_dot(cen * cen, avg)
    o3 = (cen * lax.rsqrt(var + RW_LN_EPS) * lg_ref[...] + lb_ref[...] + bon_ref[0]) * g_ref[0]
    o = (_dot(o1.astype(BF16), wo_ref[0:S5_W, :])
         + _dot(at_ref[0].astype(BF16), wo_ref[S5_W:S5_W + ATT_W, :])
         + _dot(o3.astype(BF16), wo_ref[S5_W + ATT_W:, :]))
    o_ref[0] = x_ref[0] + _pick_mod(mod_ref[0], is_ctx, 2) * o


def _outproj(x, mod, ys5, att, wkv_f, wkv_b, bonus, g, glu_w, glu_b, s5_out_g, ln_g, ln_b, w_out, tm, ctx_len):
    nb, nt, d = x.shape
    row = lambda a: a.reshape(1, -1)
    avg = _block_diag_mean(RW_W, RW_HD)
    params = (glu_w, row(glu_b), row(s5_out_g), row(ln_g), row(ln_b), avg, w_out.astype(BF16))
    full = lambda a: pl.BlockSpec(a.shape, lambda b, j: (0,) * a.ndim)
    tok = lambda w: pl.BlockSpec((1, tm, w), lambda b, j: (b, j, 0))
    return pl.pallas_call(
        functools.partial(_outproj_kernel, tm=tm, ctx_len=ctx_len),
        grid=(nb, nt // tm),
        in_specs=[tok(d), pl.BlockSpec((1, 2, N_MOD * d), lambda b, j: (b, 0, 0)),
                  tok(S5_W), tok(ATT_W), tok(RW_W), tok(RW_W), tok(RW_W), tok(RW_W)] + [full(a) for a in params],
        out_specs=tok(d),
        out_shape=jax.ShapeDtypeStruct(x.shape, F32),
        compiler_params=_params(("parallel", "parallel")),
    )(x, mod, ys5, att, wkv_f, wkv_b, bonus, g, *params)


FFN_HALO = 16
FFN_TF = 256


def _ffn_kernel(x_ref, xp_ref, xn_ref, mod_ref, g_ref, up_ref, cw_ref, cb_ref, dn_ref, o_ref,
                hs_ref, acc_ref, *, tm, ctx_len, nt):
    j = pl.program_id(1)
    f = pl.program_id(2)
    hl, tf = FFN_HALO, FFN_TF
    mod = mod_ref[0]

    def adaln(xv, first_row):
        rows = first_row + lax.broadcasted_iota(jnp.int32, (xv.shape[0], 1), 0)
        is_ctx = rows < ctx_len
        h = _rms(xv) * g_ref[...]
        return (h * (1.0 + _pick_mod(mod, is_ctx, 4)) + _pick_mod(mod, is_ctx, 3)).astype(BF16)

    @pl.when(f == 0)
    def _():
        hs_ref[0:hl, :] = adaln(xp_ref[0], j * tm - hl)
        hs_ref[hl:hl + tm, :] = adaln(x_ref[0], j * tm)
        hs_ref[hl + tm:, :] = adaln(xn_ref[0], (j + 1) * tm)
        acc_ref[...] = jnp.zeros_like(acc_ref)

    u = _dot(hs_ref[...], up_ref[0])
    n_ext = tm + 2 * hl
    row = j * tm + lax.broadcasted_iota(jnp.int32, (tm, 1), 0)
    cen = u[hl:hl + tm]
    prev = pltpu.roll(u, 1, axis=0)[hl:hl + tm]
    prev = jnp.where((row == 0) | (row == ctx_len), 0.0, prev)
    nxt = pltpu.roll(u, n_ext - 1, axis=0)[hl:hl + tm]
    nxt = jnp.where((row == ctx_len - 1) | (row == nt - 1), 0.0, nxt)
    cw = cw_ref[0]
    c = cw[0:1] * prev + cw[1:2] * cen + cw[2:3] * nxt + cb_ref[0]
    gate = c[:, :tf]
    act = gate * _sigmoid(gate) * c[:, tf:]
    acc_ref[...] += _dot(act.astype(BF16), dn_ref[...])

    @pl.when(f == pl.num_programs(2) - 1)
    def _():
        rows = j * tm + lax.broadcasted_iota(jnp.int32, (tm, 1), 0)
        o_ref[0] = x_ref[0] + _pick_mod(mod, rows < ctx_len, 5) * acc_ref[...]


def _conv_ffn(x, mod, g, up, conv_w, conv_b, down, tm, ctx_len):
    nb, nt, d = x.shape
    tf, hl = FFN_TF, FFN_HALO
    nf = D_FF // tf
    upb = up.astype(BF16)
    up_t = jnp.concatenate([upb[:, :D_FF].reshape(d, nf, tf), upb[:, D_FF:].reshape(d, nf, tf)], axis=2)
    up_t = up_t.transpose(1, 0, 2)
    pair = lambda a: jnp.concatenate([a[..., :D_FF].reshape(-1, nf, tf), a[..., D_FF:].reshape(-1, nf, tf)],
                                     axis=2).transpose(1, 0, 2)
    cw_t = pair(conv_w)
    cb_t = pair(conv_b.reshape(1, -1))
    nh = nt // hl
    return pl.pallas_call(
        functools.partial(_ffn_kernel, tm=tm, ctx_len=ctx_len, nt=nt),
        grid=(nb, nt // tm, nf),
        in_specs=[pl.BlockSpec((1, tm, d), lambda b, j, f: (b, j, 0)),
                  pl.BlockSpec((1, hl, d), lambda b, j, f: (b, jnp.maximum(j * (tm // hl) - 1, 0), 0)),
                  pl.BlockSpec((1, hl, d), lambda b, j, f: (b, jnp.minimum((j + 1) * (tm // hl), nh - 1), 0)),
                  pl.BlockSpec((1, 2, N_MOD * d), lambda b, j, f: (b, 0, 0)),
                  pl.BlockSpec((1, d), lambda b, j, f: (0, 0)),
                  pl.BlockSpec((1, d, 2 * tf), lambda b, j, f: (f, 0, 0)),
                  pl.BlockSpec((1, 3, 2 * tf), lambda b, j, f: (f, 0, 0)),
                  pl.BlockSpec((1, 1, 2 * tf), lambda b, j, f: (f, 0, 0)),
                  pl.BlockSpec((tf, d), lambda b, j, f: (f, 0))],
        out_specs=pl.BlockSpec((1, tm, d), lambda b, j, f: (b, j, 0)),
        out_shape=jax.ShapeDtypeStruct(x.shape, F32),
        scratch_shapes=[pltpu.VMEM((tm + 2 * hl, d), BF16), pltpu.VMEM((tm, d), F32)],
        compiler_params=_params(("parallel", "parallel", "arbitrary")),
    )(x, x, x, mod, g.reshape(1, d), up_t, cw_t, cb_t, down.astype(BF16))


def _final_kernel(x_ref, g_ref, o_ref):
    o_ref[0] = _rms(x_ref[0]) * g_ref[...]


def _final_norm(x, g, ctx_len, tm):
    nb, nt, d = x.shape
    off = ctx_len // tm
    return pl.pallas_call(
        _final_kernel,
        grid=(nb, (nt - ctx_len) // tm),
        in_specs=[pl.BlockSpec((1, tm, d), lambda b, j: (b, j + off, 0)),
                  pl.BlockSpec((1, d), lambda b, j: (0, 0))],
        out_specs=pl.BlockSpec((1, tm, d), lambda b, j: (b, j, 0)),
        out_shape=jax.ShapeDtypeStruct((nb, nt - ctx_len, d), F32),
        compiler_params=_params(("parallel", "parallel")),
    )(x, g.reshape(1, d))


def _forward(p, *, tm, tq, ts, tt, tfin):
    x, ctx = p["x"], p["ctx"]
    ctx_len = ctx.shape[1]
    depth = p["mod_w"].shape[0]
    xa = jnp.concatenate([ctx, x], axis=1)
    mods = _modulations(p["c"], p["c_ctx"], p["mod_w"], p["mod_b"])
    for l in range(depth):
        mod = mods[l]
        zs5, zq, zkv, zrw = _inproj(xa, mod, p["norm1_g"][l].reshape(1, -1), p["w_in"][l], tm, ctx_len)
        ops = _s5_operators(p["s5_a_re"][l], p["s5_a_im"][l], p["s5_log_step"][l], p["s5_b_re"][l],
                            p["s5_b_im"][l], p["s5_c_re"][l], p["s5_c_im"][l], p["s5_d"][l])
        ys5 = _s5_scan(zs5, ops, ctx_len)
        att = _attention(zq, zkv, p["att_qn_g"][l], p["att_kn_g"][l], p["att_out_g"][l], tq, ctx_len)
        r, v, a, w, b, kd, g, bonus = _rw_prep(zrw, p["rw_mu"][l], p["rw_w0"][l], p["rw_w2"][l], p["rw_a0"][l],
                                                 p["rw_a2"][l], p["rw_g2"][l], p["rw_k_k"][l], p["rw_k_a"][l],
                                                 p["rw_r_k"][l], tm, ctx_len)
        wkv_f, wkv_b = _rw_scan(r, v, a, w, b, kd, ts, tt, ctx_len)
        xa = _outproj(xa, mod, ys5, att, wkv_f, wkv_b, bonus, g, p["s5_glu_w"][l], p["s5_glu_b"][l],
                      p["s5_out_g"][l], p["rw_ln_g"][l], p["rw_ln_b"][l], p["w_out"][l], tm, ctx_len)
        xa = _conv_ffn(xa, mod, p["norm2_g"][l], p["ffn_up"][l], p["ffn_conv_w"][l], p["ffn_conv_b"][l],
                       p["ffn_down"][l], tm, ctx_len)
    return _final_norm(xa, p["final_g"], ctx_len, tfin)


_ARG_NAMES = ("x c ctx c_ctx norm1_g norm2_g mod_w mod_b w_in w_out s5_a_re s5_a_im s5_log_step s5_b_re s5_b_im "
              "s5_c_re s5_c_im s5_d s5_glu_w s5_glu_b s5_out_g att_qn_g att_kn_g att_out_g rw_mu rw_w0 rw_w2 "
              "rw_a0 rw_a2 rw_g2 rw_k_k rw_k_a rw_r_k rw_ln_g rw_ln_b ffn_up ffn_conv_w ffn_conv_b ffn_down "
              "final_g").split()


def kernel(x, c, ctx, c_ctx, norm1_g, norm2_g, mod_w, mod_b, w_in, w_out, s5_a_re, s5_a_im, s5_log_step, s5_b_re, s5_b_im, s5_c_re, s5_c_im, s5_d, s5_glu_w, s5_glu_b, s5_out_g, att_qn_g, att_kn_g, att_out_g, rw_mu, rw_w0, rw_w2, rw_a0, rw_a2, rw_g2, rw_k_k, rw_k_a, rw_r_k, rw_ln_g, rw_ln_b, ffn_up, ffn_conv_w, ffn_conv_b, ffn_down, final_g):
    args = (x, c, ctx, c_ctx, norm1_g, norm2_g, mod_w, mod_b, w_in, w_out, s5_a_re, s5_a_im, s5_log_step, s5_b_re,
            s5_b_im, s5_c_re, s5_c_im, s5_d, s5_glu_w, s5_glu_b, s5_out_g, att_qn_g, att_kn_g, att_out_g, rw_mu,
            rw_w0, rw_w2, rw_a0, rw_a2, rw_g2, rw_k_k, rw_k_a, rw_r_k, rw_ln_g, rw_ln_b, ffn_up, ffn_conv_w,
            ffn_conv_b, ffn_down, final_g)
    return _forward(dict(zip(_ARG_NAMES, args)), tm=768, tq=256, ts=32, tt=128, tfin=256)
```

```python
import functools
import math

import jax
import jax.numpy as jnp
from jax import lax
from jax.experimental import pallas as pl
from jax.experimental.pallas import tpu as pltpu

F32 = jnp.float32
BF16 = jnp.bfloat16
HI = lax.Precision.HIGHEST

D_MODEL = 1024
GRID_W = 64
S5_W = 256
S5_GH = 16
S5_G = 16
S5_P = 64
HEAD_DIM = 64
ATT_W = 512
N_Q = 8
GQA_REP = 4
N_KV = 2
ATT_KV = 128
ROPE_PAIRS = 16
ROPE_THETA = 10000.0
RW_W = 256
RW_HD = 64
RW_H = 4
LORA_W = 32
LORA_A = 32
LORA_G = 64
RW_IN = 960
RW_PAD = 1024
RW_LN_EPS = 64e-5
D_FF = 2816
N_MOD = 6
RMS_EPS = 1e-6

S5_T = 32
S5_CB = 8
V7X_VMEM_LIMIT = 56 * 1024 * 1024


def _params(sem, vmem=V7X_VMEM_LIMIT):
    return pltpu.CompilerParams(dimension_semantics=sem, vmem_limit_bytes=vmem)


def _rms(x):
    return x * lax.rsqrt(jnp.mean(x * x, axis=-1, keepdims=True) + RMS_EPS)


def _sigmoid(x):
    return 1.0 / (1.0 + jnp.exp(-x))


def _dot(a, b, precision=None):
    return jnp.dot(a, b, precision=precision, preferred_element_type=F32)


def _split2(x):
    hi = x.astype(BF16)
    return hi, (x - hi.astype(F32)).astype(BF16)


def _dot_x3(x, w):
    xh, xl = _split2(x)
    wh, wl = _split2(w)
    return _dot(jnp.concatenate([xh, xl, xh], axis=1), jnp.concatenate([wh, wh, wl], axis=0))


def _seg_dot(x, m):
    xh, xl = _split2(x)
    mb = m.astype(BF16)
    return _dot(jnp.concatenate([xh, xl], axis=1), jnp.concatenate([mb, mb], axis=0))


def _block_diag_mean(width, seg):
    i = jnp.arange(width)
    return ((i[:, None] // seg) == (i[None, :] // seg)).astype(F32) / seg


def _mod_kernel(c_ref, w_ref, b_ref, o_ref):
    c = c_ref[...]
    s = c * _sigmoid(c)
    o_ref[0] = _dot(s, w_ref[0], HI) + b_ref[0]


def _modulations(c, c_ctx, mod_w, mod_b):
    depth, d, n = mod_w.shape
    nb = c.shape[0]
    rows = ((nb + 1 + 7) // 8) * 8
    cc = jnp.zeros((rows, d), F32).at[:nb].set(c).at[nb].set(c_ctx)
    tn = n // 4
    out = pl.pallas_call(
        _mod_kernel,
        grid=(depth, n // tn),
        in_specs=[pl.BlockSpec((rows, d), lambda l, j: (0, 0)),
                  pl.BlockSpec((1, d, tn), lambda l, j: (l, 0, j)),
                  pl.BlockSpec((1, 1, tn), lambda l, j: (l, 0, j))],
        out_specs=pl.BlockSpec((1, rows, tn), lambda l, j: (l, 0, j)),
        out_shape=jax.ShapeDtypeStruct((depth, rows, n), F32),
        compiler_params=_params(("parallel", "parallel")),
    )(cc, mod_w, mod_b.reshape(depth, 1, n))
    lat = out[:, :nb]
    ctx = jnp.broadcast_to(out[:, nb:nb + 1], lat.shape)
    return jnp.stack([lat, ctx], axis=2)


def _pick_mod(mod, is_ctx, idx):
    lo = idx * D_MODEL
    return jnp.where(is_ctx, mod[1:2, lo:lo + D_MODEL], mod[0:1, lo:lo + D_MODEL])


def _inproj_kernel(x_ref, mod_ref, g_ref, ws_ref, wq_ref, wkv_ref, wr_ref,
                   s5_ref, q_ref, kv_ref, rw_ref, *, tm, ctx_len):
    j = pl.program_id(1)
    x = x_ref[0]
    row = j * tm + lax.broadcasted_iota(jnp.int32, (tm, 1), 0)
    is_ctx = row < ctx_len
    mod = mod_ref[0]
    h = _rms(x) * g_ref[...]
    h = h * (1.0 + _pick_mod(mod, is_ctx, 1)) + _pick_mod(mod, is_ctx, 0)
    hb = h.astype(BF16)
    s5_ref[0] = _dot(hb, ws_ref[...])
    q_ref[0] = _dot(hb, wq_ref[...])
    kv_ref[0] = _dot(hb, wkv_ref[...])
    rw_ref[0] = _dot(hb, wr_ref[...])


def _inproj(x, mod, g, w_in, tm, ctx_len):
    nb, nt, d = x.shape
    wb = w_in.astype(BF16)
    ws = wb[:, :S5_W]
    wq = wb[:, S5_W:S5_W + ATT_W]
    wkv = wb[:, S5_W + ATT_W:S5_W + ATT_W + 2 * ATT_KV]
    wr = jnp.pad(wb[:, S5_W + ATT_W + 2 * ATT_KV:], ((0, 0), (0, RW_PAD - RW_IN)))
    full = lambda a: pl.BlockSpec(a.shape, lambda b, j: (0,) * a.ndim)
    tok = lambda w: pl.BlockSpec((1, tm, w), lambda b, j: (b, j, 0))
    return pl.pallas_call(
        functools.partial(_inproj_kernel, tm=tm, ctx_len=ctx_len),
        grid=(nb, nt // tm),
        in_specs=[tok(d), pl.BlockSpec((1, 2, N_MOD * d), lambda b, j: (b, 0, 0)),
                  full(g), full(ws), full(wq), full(wkv), full(wr)],
        out_specs=[tok(S5_W), tok(ATT_W), tok(2 * ATT_KV), tok(RW_PAD)],
        out_shape=[jax.ShapeDtypeStruct((nb, nt, w), F32) for w in (S5_W, ATT_W, 2 * ATT_KV, RW_PAD)],
        compiler_params=_params(("parallel", "parallel")),
    )(x, mod, g, ws, wq, wkv, wr)


def _s5_operators(a_re, a_im, log_step, b_re, b_im, c_re, c_im, d):
    T = S5_T
    lam = lax.complex(a_re.astype(F32), a_im.astype(F32))
    step = jnp.exp(log_step.astype(F32))[..., None]
    lam_bar = jnp.exp(lam * step)
    bmat = lax.complex(b_re.astype(F32), b_im.astype(F32))
    b_bar = ((lam_bar - 1.0) / lam)[..., None] * bmat
    cmat = lax.complex(c_re.astype(F32), c_im.astype(F32))
    n = jnp.arange(T + 1, dtype=F32)
    pw = jnp.exp((lam * step)[..., None] * n)
    cp = jnp.swapaxes(cmat, -1, -2)[:, :, :, None, :] * pw[..., None]
    bp = jnp.transpose(pw, (0, 1, 3, 2))[:, :, :, None, :] * jnp.transpose(b_bar, (0, 1, 3, 2))[:, :, None]
    G, P, GH = S5_G, S5_P, S5_GH
    W = T * GH

    def ri_rows(z):
        return jnp.concatenate([jnp.real(z), -jnp.imag(z)], axis=1)

    def ri_cols(z):
        return jnp.concatenate([jnp.real(z), jnp.imag(z)], axis=2)

    bt = jnp.stack([jnp.concatenate([jnp.real(jnp.swapaxes(b_bar[di], -1, -2)),
                                     jnp.imag(jnp.swapaxes(b_bar[di], -1, -2))], axis=-1)
                    for di in range(2)], axis=1)
    cpk0 = ri_rows(cp[0, :, :, :T].reshape(G, P, W))
    cpk1 = ri_rows(cp[1, :, :, :T][:, :, ::-1].reshape(G, P, W))
    cc = jnp.concatenate([ri_rows(cp[0, :, :, 1:].reshape(G, P, W)),
                          ri_rows(cp[1, :, :, 1:][:, :, ::-1].reshape(G, P, W))], axis=1)
    bc = jnp.concatenate([ri_cols(bp[0, :, :T][:, ::-1].reshape(G, W, P)),
                          ri_cols(bp[1, :, :T].reshape(G, W, P))], axis=2)
    lt = pw[..., T]
    lam_rows = jnp.stack([jnp.concatenate([jnp.real(lt[0]), jnp.real(lt[0])], -1),
                          jnp.concatenate([-jnp.imag(lt[0]), jnp.imag(lt[0])], -1),
                          jnp.concatenate([jnp.real(lt[1]), jnp.real(lt[1])], -1),
                          jnp.concatenate([-jnp.imag(lt[1]), jnp.imag(lt[1])], -1)], axis=1)
    dvec = jnp.tile(d.astype(F32).reshape(G, 1, GH), (1, 1, T))
    return bt, cpk0, cpk1, bc, cc, lam_rows, dvec


def _s5_kernel(u_ref, bt_ref, cp0_ref, cp1_ref, bc_ref, cc_ref, lam_ref, d_ref, y_ref,
               kt_ref, spf_ref, spb_ref, ef_ref, eb_ref, *, n_ctx, n_chunks, nb):
    T, GH, P2 = S5_T, S5_GH, 2 * S5_P
    W = T * GH
    krow0 = _dot(bt_ref[0, 0], cp0_ref[0], HI)
    krow1 = _dot(bt_ref[0, 1], cp1_ref[0], HI)
    lane = lax.broadcasted_iota(jnp.int32, (GH, W), 1)
    for s in range(T):
        f = krow0 if s == 0 else pltpu.roll(krow0, s * GH, axis=1)
        f = jnp.where(lane >= s * GH, f, 0.0)
        sh = (T - 1 - s) * GH
        b = krow1 if sh == 0 else pltpu.roll(krow1, W - sh, axis=1)
        b = jnp.where(lane < (s + 1) * GH, b, 0.0)
        kt_ref[s * GH:(s + 1) * GH, :] = f + b
    u = u_ref[0]
    y = u * d_ref[0] + _dot_x3(u, kt_ref[...])
    e = _dot_x3(u, bc_ref[0])
    ef_ref[...] = e[:, :P2]
    eb_ref[...] = e[:, P2:]
    lam = lam_ref[0]

    def run(order, e_ref, sp_ref, ra, rb):
        s = jnp.zeros((nb, P2), F32)
        for c in order:
            rows = pl.ds((c // S5_CB) * S5_CB * nb + c % S5_CB, nb, stride=S5_CB)
            sp_ref[rows, :] = s
            s = ra * s + rb * pltpu.roll(s, S5_P, axis=1) + e_ref[rows, :]

    run(list(range(n_chunks)), ef_ref, spf_ref, lam[0:1], lam[1:2])
    run(list(range(n_ctx - 1, -1, -1)) + list(range(n_chunks - 1, n_ctx - 1, -1)), eb_ref, spb_ref,
        lam[2:3], lam[3:4])
    sp = jnp.concatenate([spf_ref[...], spb_ref[...]], axis=1)
    y_ref[0] = y + _dot_x3(sp, cc_ref[0])


S5_GL = 128 // S5_GH


def _s5_in_kernel(z_ref, u_ref, *, nb):
    T, GH, CB = S5_T, S5_GH, S5_CB

    def one_batch(b, carry):
        rows = pl.ds(pl.multiple_of(b * CB, CB), CB)
        for t in range(T):
            zt = z_ref[b, pl.ds(t, CB, stride=T), :]
            for g in range(S5_GL):
                u_ref[g, rows, t * GH:(t + 1) * GH] = zt[:, g * GH:(g + 1) * GH]
        return carry

    lax.fori_loop(0, nb, one_batch, 0)


def _s5_out_kernel(y_ref, o_ref, row_ref, *, nb):
    T, GH, CB = S5_T, S5_GH, S5_CB

    def one_batch(b, carry):
        rows = pl.ds(pl.multiple_of(b * CB, CB), CB)
        for t in range(T):
            for g in range(S5_GL):
                row_ref[:, g * GH:(g + 1) * GH] = y_ref[g, rows, t * GH:(t + 1) * GH]
            o_ref[b, pl.ds(t, CB, stride=T), :] = row_ref[...]
        return carry

    lax.fori_loop(0, nb, one_batch, 0)


def _s5_scan(zs5, ops, ctx_len):
    nb, nt, _ = zs5.shape
    T, G, GH, CB, GL = S5_T, S5_G, S5_GH, S5_CB, S5_GL
    W = T * GH
    n_chunks = nt // T
    n_blocks = n_chunks // CB
    rows = n_chunks * nb
    relayout_grid = (n_blocks, G // GL)
    nat = pl.BlockSpec((nb, CB * T, GL * GH), lambda i, h: (0, i, h))
    chunked = pl.BlockSpec((GL, nb * CB, W), lambda i, h: (h, i, 0))
    u = pl.pallas_call(
        functools.partial(_s5_in_kernel, nb=nb),
        grid=relayout_grid,
        in_specs=[nat],
        out_specs=chunked,
        out_shape=jax.ShapeDtypeStruct((G, rows, W), F32),
        compiler_params=_params(("parallel", "parallel")),
    )(zs5)
    bt, cpk0, cpk1, bc, cc, lam_rows, dvec = ops
    per_g = lambda a: pl.BlockSpec((1,) + a.shape[1:], lambda g: (g,) + (0,) * (a.ndim - 1))
    y = pl.pallas_call(
        functools.partial(_s5_kernel, n_ctx=ctx_len // T, n_chunks=n_chunks, nb=nb),
        grid=(G,),
        in_specs=[per_g(u), per_g(bt), per_g(cpk0), per_g(cpk1), per_g(bc), per_g(cc),
                  per_g(lam_rows), per_g(dvec)],
        out_specs=per_g(u),
        out_shape=jax.ShapeDtypeStruct(u.shape, F32),
        scratch_shapes=[pltpu.VMEM((W, W), F32)] + [pltpu.VMEM((rows, 2 * S5_P), F32)] * 4,
        compiler_params=_params(("parallel",)),
    )(u, bt, cpk0, cpk1, bc, cc, lam_rows, dvec)
    return pl.pallas_call(
        functools.partial(_s5_out_kernel, nb=nb),
        grid=relayout_grid,
        in_specs=[chunked],
        out_specs=nat,
        out_shape=jax.ShapeDtypeStruct(zs5.shape, F32),
        scratch_shapes=[pltpu.VMEM((CB, GL * GH), F32)],
        compiler_params=_params(("parallel", "parallel")),
    )(y)


def _rope_tables(nt, ctx_len, width):
    tok = jnp.arange(nt - ctx_len, dtype=jnp.int32)
    pos_row = (tok // GRID_W).astype(F32)
    pos_col = (tok % GRID_W).astype(F32)
    inv = ROPE_THETA ** (-jnp.arange(ROPE_PAIRS, dtype=F32) / ROPE_PAIRS)
    ang_r = pos_row[:, None] * inv
    ang_c = pos_col[:, None] * inv
    ang = jnp.concatenate([ang_r, ang_r, ang_c, ang_c], axis=1)
    sign = jnp.tile(jnp.concatenate([-jnp.ones(ROPE_PAIRS, F32), jnp.ones(ROPE_PAIRS, F32)]), 2)
    cos = jnp.concatenate([jnp.ones((ctx_len, HEAD_DIM), F32), jnp.cos(ang)], axis=0)
    sin = jnp.concatenate([jnp.zeros((ctx_len, HEAD_DIM), F32), jnp.sin(ang) * sign], axis=0)
    reps = width // HEAD_DIM
    return jnp.tile(cos, (1, reps)), jnp.tile(sin, (1, reps))


def _head_norm_rope(t, pmat, g, cos, sin):
    width = t.shape[-1]
    ms = _seg_dot(t * t, pmat)
    tn = t * lax.rsqrt(ms + RMS_EPS) * g
    lane = lax.broadcasted_iota(jnp.int32, tn.shape, 1)
    first = (lane % (2 * ROPE_PAIRS)) < ROPE_PAIRS
    partner = jnp.where(first, pltpu.roll(tn, width - ROPE_PAIRS, axis=1), pltpu.roll(tn, ROPE_PAIRS, axis=1))
    return tn * cos + partner * sin


def _attn_kernel(q_ref, kv_ref, cq_ref, sq_ref, ck_ref, sk_ref, qg_ref, kg_ref, og_ref, pq_ref, pk_ref,
                 o_ref, kn_ref, vb_ref, *, tq, ctx_len, nk):
    j = pl.program_id(1)

    @pl.when(j == 0)
    def _():
        kv = kv_ref[0]
        kr = _head_norm_rope(kv[:, :ATT_KV], pk_ref[...], kg_ref[...], ck_ref[...], sk_ref[...])
        ones = jnp.ones((kv.shape[0], HEAD_DIM), BF16)
        for g in range(N_KV):
            kn_ref[g] = kr[:, g * HEAD_DIM:(g + 1) * HEAD_DIM].astype(BF16)
            vb_ref[g, :, :HEAD_DIM] = kv[:, ATT_KV + g * HEAD_DIM:ATT_KV + (g + 1) * HEAD_DIM].astype(BF16)
            vb_ref[g, :, HEAD_DIM:] = ones

    qr = _head_norm_rope(q_ref[0], pq_ref[...], qg_ref[...], cq_ref[...], sq_ref[...]) * (
        HEAD_DIM ** -0.5 * math.log2(math.e))

    def attend(n_keys):
        outs = []
        for h in range(N_Q):
            g = h // GQA_REP
            qh = qr[:, h * HEAD_DIM:(h + 1) * HEAD_DIM].astype(BF16)
            s = lax.dot_general(qh, kn_ref[g, :n_keys, :], (((1,), (1,)), ((), ())),
                                preferred_element_type=F32)
            p = jnp.exp2(s - jnp.max(s, axis=-1, keepdims=True))
            ov = _dot(p.astype(BF16), vb_ref[g, :n_keys, :])
            outs.append(ov[:, :HEAD_DIM] / ov[:, HEAD_DIM:])
        o = jnp.concatenate(outs, axis=1)
        o_ref[0] = _rms(o) * og_ref[...]

    @pl.when(j * tq < ctx_len)
    def _():
        attend(ctx_len)

    @pl.when(j * tq >= ctx_len)
    def _():
        attend(nk)


def _attention(zq, zkv, qn_g, kn_g, out_g, tq, ctx_len):
    nb, nt, _ = zq.shape
    cq, sq = _rope_tables(nt, ctx_len, ATT_W)
    ck, sk = cq[:, :ATT_KV], sq[:, :ATT_KV]
    qg = jnp.tile(qn_g, N_Q).reshape(1, ATT_W)
    kg = jnp.tile(kn_g, N_KV).reshape(1, ATT_KV)
    pq = _block_diag_mean(ATT_W, HEAD_DIM)
    pk = _block_diag_mean(ATT_KV, HEAD_DIM)
    full = lambda a: pl.BlockSpec(a.shape, lambda b, j: (0,) * a.ndim)
    return pl.pallas_call(
        functools.partial(_attn_kernel, tq=tq, ctx_len=ctx_len, nk=nt),
        grid=(nb, nt // tq),
        in_specs=[pl.BlockSpec((1, tq, ATT_W), lambda b, j: (b, j, 0)),
                  pl.BlockSpec((1, nt, 2 * ATT_KV), lambda b, j: (b, 0, 0)),
                  pl.BlockSpec((tq, ATT_W), lambda b, j: (j, 0)),
                  pl.BlockSpec((tq, ATT_W), lambda b, j: (j, 0)),
                  full(ck), full(sk), full(qg), full(kg),
                  pl.BlockSpec((1, ATT_W), lambda b, j: (0, 0)), full(pq), full(pk)],
        out_specs=pl.BlockSpec((1, tq, ATT_W), lambda b, j: (b, j, 0)),
        out_shape=jax.ShapeDtypeStruct((nb, nt, ATT_W), F32),
        scratch_shapes=[pltpu.VMEM((N_KV, nt, HEAD_DIM), BF16), pltpu.VMEM((N_KV, nt, 2 * HEAD_DIM), BF16)],
        compiler_params=_params(("parallel", "arbitrary")),
    )(zq, zkv, cq, sq, ck, sk, qg, kg, out_g.reshape(1, ATT_W), pq, pk)


def _rw_prep_kernel(z_ref, zp_ref, zn_ref, mu_ref, wl_ref, w0_ref, a0_ref, kk_ref, ka_ref, rk_ref, ones_ref,
                    r_ref, v_ref, a_ref, w_ref, b_ref, kd_ref, g_ref, bon_ref, *, tm, ctx_len, nt):
    j = pl.program_id(1)
    z = z_ref[0]
    row_l = lax.broadcasted_iota(jnp.int32, (tm, 1), 0)
    row = j * tm + row_l
    prev = jnp.where(row_l == 0, zp_ref[0, 7:8, :], pltpu.roll(z, 1, axis=0))
    prev = jnp.where((row == 0) | (row == ctx_len), 0.0, prev)
    nxt = jnp.where(row_l == tm - 1, zn_ref[0, 0:1, :], pltpu.roll(z, tm - 1, axis=0))
    nxt = jnp.where((row == ctx_len - 1) | (row == nt - 1), 0.0, nxt)
    zs = z + mu_ref[...] * (0.5 * (prev + nxt) - z)
    r = zs[:, 0:RW_W]
    k = zs[:, RW_W:2 * RW_W]
    v = zs[:, 2 * RW_W:3 * RW_W]
    lo = zs[:, 3 * RW_W:]
    lane = lax.broadcasted_iota(jnp.int32, lo.shape, 1)
    lo = jnp.where(lane < 2 * LORA_W, jnp.tanh(lo),
                   jnp.where(lane < 2 * LORA_W + 2 * LORA_A, lo, _sigmoid(lo)))
    proj = _dot_x3(lo, wl_ref[...])
    ones = ones_ref[...]
    kk = k * kk_ref[...]
    nrm = jnp.sqrt(_seg_dot(kk * kk, ones))
    kk = kk / jnp.maximum(nrm, 1e-12)
    r_ref[0] = r
    v_ref[0] = v
    a_ref[0] = -kk
    g_ref[0] = proj[:, 4 * RW_W:5 * RW_W]
    bon = jnp.zeros_like(r)
    for di in range(2):
        pre = w0_ref[di:di + 1, :] + proj[:, di * RW_W:(di + 1) * RW_W]
        w_log = -(jnp.maximum(-pre, 0.0) + jnp.log(1.0 + jnp.exp(-jnp.abs(pre)))) - 0.5
        w_ref[di, 0] = jnp.exp(-jnp.exp(w_log))
        iclr = _sigmoid(a0_ref[di:di + 1, :] + proj[:, (2 + di) * RW_W:(3 + di) * RW_W])
        kd = k * (1.0 + (iclr - 1.0) * ka_ref[...])
        kd_ref[di, 0] = kd
        b_ref[di, 0] = kk * iclr
        bon = bon + _seg_dot(r * kd * rk_ref[...], ones) * v
    bon_ref[0] = bon


def _rw_prep(zrw, mu, w0, w2, a0, a2, g2, k_k, k_a, r_k, tm, ctx_len):
    nb, nt, _ = zrw.shape
    wl = jnp.zeros((RW_PAD - 3 * RW_W, 5 * RW_W), F32)
    wl = wl.at[0:LORA_W, 0:RW_W].set(w2[0]).at[LORA_W:2 * LORA_W, RW_W:2 * RW_W].set(w2[1])
    o = 2 * LORA_W
    wl = wl.at[o:o + LORA_A, 2 * RW_W:3 * RW_W].set(a2[0]).at[o + LORA_A:o + 2 * LORA_A, 3 * RW_W:4 * RW_W].set(a2[1])
    o = 2 * LORA_W + 2 * LORA_A
    wl = wl.at[o:o + LORA_G, 4 * RW_W:5 * RW_W].set(g2)
    mu_p = jnp.pad(mu, (0, RW_PAD - RW_IN)).reshape(1, RW_PAD)
    ones = _block_diag_mean(RW_W, RW_HD) * RW_HD
    row = lambda a: a.reshape(1, RW_W)
    full = lambda a: pl.BlockSpec(a.shape, lambda b, j: (0,) * a.ndim)
    tok = pl.BlockSpec((1, tm, RW_W), lambda b, j: (b, j, 0))
    tok2 = pl.BlockSpec((2, 1, tm, RW_W), lambda b, j: (0, b, j, 0))
    one = jax.ShapeDtypeStruct((nb, nt, RW_W), F32)
    two = jax.ShapeDtypeStruct((2, nb, nt, RW_W), F32)
    n8 = nt // 8
    args = (zrw, zrw, zrw, mu_p, wl, w0, a0, row(k_k), row(k_a), row(r_k), ones)
    return pl.pallas_call(
        functools.partial(_rw_prep_kernel, tm=tm, ctx_len=ctx_len, nt=nt),
        grid=(nb, nt // tm),
        in_specs=[pl.BlockSpec((1, tm, RW_PAD), lambda b, j: (b, j, 0)),
                  pl.BlockSpec((1, 8, RW_PAD), lambda b, j: (b, jnp.maximum(j * (tm // 8) - 1, 0), 0)),
                  pl.BlockSpec((1, 8, RW_PAD), lambda b, j: (b, jnp.minimum((j + 1) * (tm // 8), n8 - 1), 0)),
                  ] + [full(a) for a in args[3:]],
        out_specs=[tok, tok, tok, tok2, tok2, tok2, tok, tok],
        out_shape=[one, one, one, two, two, two, one, one],
        compiler_params=_params(("parallel", "parallel")),
    )(*args)


RW_VH = RW_HD // 2
RW_SLOTS = 4


def _to_chains_kernel(*refs, nb, tt):
    *x_refs, o_ref, scr_ref = refs
    for j, ref in enumerate(x_refs):
        for b in range(nb):
            scr_ref[(j * nb + b) * RW_W:(j * nb + b + 1) * RW_W, :] = ref[0, b].T
    rows = len(x_refs) * nb * RW_H
    for k in range(RW_HD):
        slab = scr_ref[pl.ds(k, rows, stride=RW_HD), :]
        slab = jnp.concatenate([slab] * (RW_SLOTS // len(x_refs)), axis=0)
        o_ref[pl.ds(k, tt, stride=RW_HD), :] = slab.T


def _to_chains(xs, tt):
    _, nb, nt, _ = xs[0][0].shape
    lanes = RW_SLOTS * nb * RW_H
    spec = lambda di: pl.BlockSpec((1, nb, tt, RW_W), lambda i: (di, 0, i, 0))
    return pl.pallas_call(
        functools.partial(_to_chains_kernel, nb=nb, tt=tt),
        grid=(nt // tt,),
        in_specs=[spec(di) for _, di in xs],
        out_specs=pl.BlockSpec((tt * RW_HD, lanes), lambda i: (i, 0)),
        out_shape=jax.ShapeDtypeStruct((nt * RW_HD, lanes), F32),
        scratch_shapes=[pltpu.VMEM((lanes * RW_HD, tt), F32)],
        compiler_params=_params(("parallel",)),
    )(*[x for x, _ in xs])


def _rw_scan_kernel(fa_ref, fc_ref, ba_ref, bc_ref, yf_ref, yb_ref, s_ref, vec_ref, *, ts, lanes):
    @pl.when(pl.program_id(0) == 0)
    def _():
        s_ref[...] = jnp.zeros_like(s_ref)

    lane = lax.broadcasted_iota(jnp.int32, (RW_HD, lanes), 1)
    q = lanes // 4
    even = (lane // q) % 2 == 0
    low = lane < 2 * q
    low_v = lax.broadcasted_iota(jnp.int32, (RW_VH, lanes), 1) < 2 * q

    def pairs(x):
        xs = pltpu.roll(x, 2 * q, axis=1)
        return jnp.where(low, x, xs), jnp.where(low, xs, x)

    def merge(f, b):
        return (jnp.where(even, f, pltpu.roll(b, q, axis=1)),
                jnp.where(even, pltpu.roll(f, lanes - q, axis=1), b))

    def prepare(i, slot):
        fr = pl.ds(pl.multiple_of(i * RW_HD, RW_HD), RW_HD)
        br = pl.ds(pl.multiple_of((ts - 1 - i) * RW_HD, RW_HD), RW_HD)
        f01, f23 = pairs(fa_ref[fr, :])
        b01, b23 = pairs(ba_ref[br, :])
        vec_ref[slot, 0], vec_ref[slot, 1] = merge(f01, b01)
        vec_ref[slot, 2], vec_ref[slot, 3] = merge(f23, b23)
        vec_ref[slot, 4], vk = merge(fc_ref[fr, :], bc_ref[br, :])
        vec_ref[slot, 5, :RW_VH] = jnp.where(low_v, vk[:RW_VH], vk[RW_VH:])

    def update(i, slot):
        for vi in range(RW_VH):
            s = s_ref[vi]
            sa = jnp.sum(s * vec_ref[slot, 3], axis=0, keepdims=True)
            s = s * vec_ref[slot, 0] + sa * vec_ref[slot, 1] + vec_ref[slot, 5, vi:vi + 1, :] * vec_ref[slot, 2]
            s_ref[vi] = s
            y = jnp.sum(s * vec_ref[slot, 4], axis=0, keepdims=True)
            yf_ref[pl.ds(i * RW_VH + vi, 1), :] = y
            yb_ref[pl.ds((ts - 1 - i) * RW_VH + vi, 1), :] = y

    prepare(0, 0)

    def two_steps(h, carry):
        i = 2 * h
        prepare(i + 1, 1)
        update(i, 0)
        prepare(jnp.minimum(i + 2, ts - 1), 0)
        update(i + 1, 1)
        return carry

    lax.fori_loop(0, ts // 2, two_steps, 0)


def _from_chains_kernel(yf_ref, yb_ref, of_ref, ob_ref, scr_ref, *, nb, tt):
    q = nb * RW_H
    for d, (y_ref, o_ref) in enumerate(((yf_ref, of_ref), (yb_ref, ob_ref))):
        for vi in range(RW_VH):
            st = y_ref[pl.ds(vi, tt, stride=RW_VH), :].T
            for vh in range(2):
                src = (vh * 2 + d) * q
                scr_ref[pl.ds(vh * RW_VH + vi, q, stride=RW_HD), :] = st[src:src + q, :]
        for b in range(nb):
            o_ref[b] = scr_ref[b * RW_W:(b + 1) * RW_W, :].T


def _rw_scan(r, v, a, w, b, kd, ts, tt, ctx_len):
    nb, nt, _ = r.shape
    lanes = RW_SLOTS * nb * RW_H
    one = lambda x: x.reshape(1, nb, nt, RW_W)
    r1, v1, a1 = one(r), one(v), one(a)
    fa = _to_chains([(w, 0), (b, 0), (kd, 0), (a1, 0)], tt)
    ba = _to_chains([(w, 1), (b, 1), (kd, 1), (a1, 0)], tt)
    fc = _to_chains([(r1, 0), (v1, 0)], tt)
    n_ctx = ctx_len // ts
    n_all = nt // ts
    fwd = lambda g: (g, 0)
    bwd = lambda g: (jnp.where(g < n_ctx, n_ctx - 1 - g, n_all - 1 - g + n_ctx), 0)
    kin = lambda m: pl.BlockSpec((ts * RW_HD, lanes), m)
    yout = lambda m: pl.BlockSpec((ts * RW_VH, lanes), m)
    yshape = jax.ShapeDtypeStruct((nt * RW_VH, lanes), F32)
    yf, yb = pl.pallas_call(
        functools.partial(_rw_scan_kernel, ts=ts, lanes=lanes),
        grid=(n_all,),
        in_specs=[kin(fwd), kin(fwd), kin(bwd), kin(bwd)],
        out_specs=[yout(fwd), yout(bwd)],
        out_shape=[yshape, yshape],
        scratch_shapes=[pltpu.VMEM((RW_VH, RW_HD, lanes), F32), pltpu.VMEM((2, 6, RW_HD, lanes), F32)],
        compiler_params=_params(("arbitrary",)),
    )(fa, fc, ba, fc)
    nat = jax.ShapeDtypeStruct((nb, nt, RW_W), F32)
    return pl.pallas_call(
        functools.partial(_from_chains_kernel, nb=nb, tt=tt),
        grid=(nt // tt,),
        in_specs=[pl.BlockSpec((tt * RW_VH, lanes), lambda i: (i, 0))] * 2,
        out_specs=[pl.BlockSpec((nb, tt, RW_W), lambda i: (0, i, 0))] * 2,
        out_shape=[nat, nat],
        scratch_shapes=[pltpu.VMEM((nb * RW_W, tt), F32)],
        compiler_params=_params(("parallel",)),
    )(yf, yb)


def _outproj_kernel(x_ref, mod_ref, ys_ref, at_ref, wf_ref, wb_ref, bon_ref, g_ref,
                    gw_ref, gb_ref, sg_ref, lg_ref, lb_ref, avg_ref, wo_ref, o_ref, *, tm, ctx_len):
    j = pl.program_id(1)
    row = j * tm + lax.broadcasted_iota(jnp.int32, (tm, 1), 0)
    is_ctx = row < ctx_len
    a = jax.nn.gelu(ys_ref[0])
    o1 = _rms(a * _sigmoid(_dot_x3(a, gw_ref[...]) + gb_ref[...])) * sg_ref[...]
    wkv = wf_ref[0] + wb_ref[0]
    avg = avg_ref[...]
    cen = wkv - _seg_dot(wkv, avg)
    var = _seg_dot(cen * cen, avg)
    o3 = (cen * lax.rsqrt(var + RW_LN_EPS) * lg_ref[...] + lb_ref[...] + bon_ref[0]) * g_ref[0]
    o = (_dot(o1.astype(BF16), wo_ref[0:S5_W, :])
         + _dot(at_ref[0].astype(BF16), wo_ref[S5_W:S5_W + ATT_W, :])
         + _dot(o3.astype(BF16), wo_ref[S5_W + ATT_W:, :]))
    o_ref[0] = x_ref[0] + _pick_mod(mod_ref[0], is_ctx, 2) * o


def _outproj(x, mod, ys5, att, wkv_f, wkv_b, bonus, g, glu_w, glu_b, s5_out_g, ln_g, ln_b, w_out, tm, ctx_len):
    nb, nt, d = x.shape
    row = lambda a: a.reshape(1, -1)
    avg = _block_diag_mean(RW_W, RW_HD)
    params = (glu_w, row(glu_b), row(s5_out_g), row(ln_g), row(ln_b), avg, w_out.astype(BF16))
    full = lambda a: pl.BlockSpec(a.shape, lambda b, j: (0,) * a.ndim)
    tok = lambda w: pl.BlockSpec((1, tm, w), lambda b, j: (b, j, 0))
    return pl.pallas_call(
        functools.partial(_outproj_kernel, tm=tm, ctx_len=ctx_len),
        grid=(nb, nt // tm),
        in_specs=[tok(d), pl.BlockSpec((1, 2, N_MOD * d), lambda b, j: (b, 0, 0)),
                  tok(S5_W), tok(ATT_W), tok(RW_W), tok(RW_W), tok(RW_W), tok(RW_W)] + [full(a) for a in params],
        out_specs=tok(d),
        out_shape=jax.ShapeDtypeStruct(x.shape, F32),
        compiler_params=_params(("parallel", "parallel")),
    )(x, mod, ys5, att, wkv_f, wkv_b, bonus, g, *params)


FFN_HALO = 16
FFN_TF = 256


def _ffn_kernel(x_ref, xp_ref, xn_ref, mod_ref, g_ref, upg_ref, upv_ref, cwg_ref, cwv_ref, cbg_ref, cbv_ref,
                dn_ref, o_ref, hs_ref, acc_ref, *, tm, ctx_len, nt):
    j = pl.program_id(1)
    f = pl.program_id(2)
    hl, tf = FFN_HALO, FFN_TF
    mod = mod_ref[0]

    def adaln(xv, first_row):
        rows = first_row + lax.broadcasted_iota(jnp.int32, (xv.shape[0], 1), 0)
        is_ctx = rows < ctx_len
        h = _rms(xv) * g_ref[...]
        return (h * (1.0 + _pick_mod(mod, is_ctx, 4)) + _pick_mod(mod, is_ctx, 3)).astype(BF16)

    @pl.when(f == 0)
    def _():
        hs_ref[0:hl, :] = adaln(xp_ref[0], j * tm - hl)
        hs_ref[hl:hl + tm, :] = adaln(x_ref[0], j * tm)
        hs_ref[hl + tm:, :] = adaln(xn_ref[0], (j + 1) * tm)
        acc_ref[...] = jnp.zeros_like(acc_ref)

    n_ext = tm + 2 * hl
    row = j * tm + lax.broadcasted_iota(jnp.int32, (tm, 1), 0)
    no_prev = (row == 0) | (row == ctx_len)
    no_next = (row == ctx_len - 1) | (row == nt - 1)
    hs = hs_ref[...]

    def conv(up_ref, cw_ref, cb_ref):
        u = _dot(hs, up_ref[...])
        prev = jnp.where(no_prev, 0.0, pltpu.roll(u, 1, axis=0)[hl:hl + tm])
        nxt = jnp.where(no_next, 0.0, pltpu.roll(u, n_ext - 1, axis=0)[hl:hl + tm])
        cw = cw_ref[...]
        return cw[0:1] * prev + cw[1:2] * u[hl:hl + tm] + cw[2:3] * nxt + cb_ref[...]

    half = 0.5 * conv(upg_ref, cwg_ref, cbg_ref)
    act = (half + half * jnp.tanh(half)) * conv(upv_ref, cwv_ref, cbv_ref)
    acc_ref[...] += _dot(act.astype(BF16), dn_ref[...])

    @pl.when(f == pl.num_programs(2) - 1)
    def _():
        rows = j * tm + lax.broadcasted_iota(jnp.int32, (tm, 1), 0)
        o_ref[0] = x_ref[0] + _pick_mod(mod, rows < ctx_len, 5) * acc_ref[...]


def _conv_ffn(x, mod, g, up, conv_w, conv_b, down, tm, ctx_len):
    nb, nt, d = x.shape
    tf, hl = FFN_TF, FFN_HALO
    nf = D_FF // tf
    upb = up.astype(BF16)
    cb = conv_b.reshape(1, -1)
    nh = nt // hl
    cols = lambda rows: [pl.BlockSpec((rows, tf), lambda b, j, f: (0, f)),
                         pl.BlockSpec((rows, tf), lambda b, j, f: (0, nf + f))]
    return pl.pallas_call(
        functools.partial(_ffn_kernel, tm=tm, ctx_len=ctx_len, nt=nt),
        grid=(nb, nt // tm, nf),
        in_specs=[pl.BlockSpec((1, tm, d), lambda b, j, f: (b, j, 0)),
                  pl.BlockSpec((1, hl, d), lambda b, j, f: (b, jnp.maximum(j * (tm // hl) - 1, 0), 0)),
                  pl.BlockSpec((1, hl, d), lambda b, j, f: (b, jnp.minimum((j + 1) * (tm // hl), nh - 1), 0)),
                  pl.BlockSpec((1, 2, N_MOD * d), lambda b, j, f: (b, 0, 0)),
                  pl.BlockSpec((1, d), lambda b, j, f: (0, 0))] + cols(d) + cols(3) + cols(1) + [
                  pl.BlockSpec((tf, d), lambda b, j, f: (f, 0))],
        out_specs=pl.BlockSpec((1, tm, d), lambda b, j, f: (b, j, 0)),
        out_shape=jax.ShapeDtypeStruct(x.shape, F32),
        scratch_shapes=[pltpu.VMEM((tm + 2 * hl, d), BF16), pltpu.VMEM((tm, d), F32)],
        compiler_params=_params(("parallel", "parallel", "arbitrary")),
    )(x, x, x, mod, g.reshape(1, d), upb, upb, conv_w, conv_w, cb, cb, down.astype(BF16))


def _final_kernel(x_ref, g_ref, o_ref):
    o_ref[0] = _rms(x_ref[0]) * g_ref[...]


def _final_norm(x, g, ctx_len, tm):
    nb, nt, d = x.shape
    off = ctx_len // tm
    return pl.pallas_call(
        _final_kernel,
        grid=(nb, (nt - ctx_len) // tm),
        in_specs=[pl.BlockSpec((1, tm, d), lambda b, j: (b, j + off, 0)),
                  pl.BlockSpec((1, d), lambda b, j: (0, 0))],
        out_specs=pl.BlockSpec((1, tm, d), lambda b, j: (b, j, 0)),
        out_shape=jax.ShapeDtypeStruct((nb, nt - ctx_len, d), F32),
        compiler_params=_params(("parallel", "parallel")),
    )(x, g.reshape(1, d))


def _forward(p, *, tm, tq, ts, tt, tfin):
    x, ctx = p["x"], p["ctx"]
    ctx_len = ctx.shape[1]
    depth = p["mod_w"].shape[0]
    xa = jnp.concatenate([ctx, x], axis=1)
    mods = _modulations(p["c"], p["c_ctx"], p["mod_w"], p["mod_b"])
    for l in range(depth):
        mod = mods[l]
        zs5, zq, zkv, zrw = _inproj(xa, mod, p["norm1_g"][l].reshape(1, -1), p["w_in"][l], tm, ctx_len)
        ops = _s5_operators(p["s5_a_re"][l], p["s5_a_im"][l], p["s5_log_step"][l], p["s5_b_re"][l],
                            p["s5_b_im"][l], p["s5_c_re"][l], p["s5_c_im"][l], p["s5_d"][l])
        ys5 = _s5_scan(zs5, ops, ctx_len)
        att = _attention(zq, zkv, p["att_qn_g"][l], p["att_kn_g"][l], p["att_out_g"][l], tq, ctx_len)
        r, v, a, w, b, kd, g, bonus = _rw_prep(zrw, p["rw_mu"][l], p["rw_w0"][l], p["rw_w2"][l], p["rw_a0"][l],
                                                 p["rw_a2"][l], p["rw_g2"][l], p["rw_k_k"][l], p["rw_k_a"][l],
                                                 p["rw_r_k"][l], tm, ctx_len)
        wkv_f, wkv_b = _rw_scan(r, v, a, w, b, kd, ts, tt, ctx_len)
        xa = _outproj(xa, mod, ys5, att, wkv_f, wkv_b, bonus, g, p["s5_glu_w"][l], p["s5_glu_b"][l],
                      p["s5_out_g"][l], p["rw_ln_g"][l], p["rw_ln_b"][l], p["w_out"][l], tm, ctx_len)
        xa = _conv_ffn(xa, mod, p["norm2_g"][l], p["ffn_up"][l], p["ffn_conv_w"][l], p["ffn_conv_b"][l],
                       p["ffn_down"][l], tm, ctx_len)
    return _final_norm(xa, p["final_g"], ctx_len, tfin)


_ARG_NAMES = ("x c ctx c_ctx norm1_g norm2_g mod_w mod_b w_in w_out s5_a_re s5_a_im s5_log_step s5_b_re s5_b_im "
              "s5_c_re s5_c_im s5_d s5_glu_w s5_glu_b s5_out_g att_qn_g att_kn_g att_out_g rw_mu rw_w0 rw_w2 "
              "rw_a0 rw_a2 rw_g2 rw_k_k rw_k_a rw_r_k rw_ln_g rw_ln_b ffn_up ffn_conv_w ffn_conv_b ffn_down "
              "final_g").split()


def kernel(x, c, ctx, c_ctx, norm1_g, norm2_g, mod_w, mod_b, w_in, w_out, s5_a_re, s5_a_im, s5_log_step, s5_b_re, s5_b_im, s5_c_re, s5_c_im, s5_d, s5_glu_w, s5_glu_b, s5_out_g, att_qn_g, att_kn_g, att_out_g, rw_mu, rw_w0, rw_w2, rw_a0, rw_a2, rw_g2, rw_k_k, rw_k_a, rw_r_k, rw_ln_g, rw_ln_b, ffn_up, ffn_conv_w, ffn_conv_b, ffn_down, final_g):
    args = (x, c, ctx, c_ctx, norm1_g, norm2_g, mod_w, mod_b, w_in, w_out, s5_a_re, s5_a_im, s5_log_step, s5_b_re,
            s5_b_im, s5_c_re, s5_c_im, s5_d, s5_glu_w, s5_glu_b, s5_out_g, att_qn_g, att_kn_g, att_out_g, rw_mu,
            rw_w0, rw_w2, rw_a0, rw_a2, rw_g2, rw_k_k, rw_k_a, rw_r_k, rw_ln_g, rw_ln_b, ffn_up, ffn_conv_w,
            ffn_conv_b, ffn_down, final_g)
    return _forward(dict(zip(_ARG_NAMES, args)), tm=768, tq=256, ts=32, tt=128, tfin=256)
```

```python
import functools
import math

import jax
import jax.numpy as jnp
from jax import lax
from jax.experimental import pallas as pl
from jax.experimental.pallas import tpu as pltpu

F32 = jnp.float32
BF16 = jnp.bfloat16
HI = lax.Precision.HIGHEST

D_MODEL = 1024
GRID_W = 64
S5_W = 256
S5_GH = 16
S5_G = 16
S5_P = 64
HEAD_DIM = 64
ATT_W = 512
N_Q = 8
GQA_REP = 4
N_KV = 2
ATT_KV = 128
ROPE_PAIRS = 16
ROPE_THETA = 10000.0
RW_W = 256
RW_HD = 64
RW_H = 4
LORA_W = 32
LORA_A = 32
LORA_G = 64
RW_IN = 960
RW_PAD = 1024
RW_LN_EPS = 64e-5
D_FF = 2816
N_MOD = 6
RMS_EPS = 1e-6

S5_T = 32
S5_CB = 8
V7X_VMEM_LIMIT = 56 * 1024 * 1024


def _params(sem, vmem=V7X_VMEM_LIMIT):
    return pltpu.CompilerParams(dimension_semantics=sem, vmem_limit_bytes=vmem)


def _rms(x):
    return x * lax.rsqrt(jnp.mean(x * x, axis=-1, keepdims=True) + RMS_EPS)


def _sigmoid(x):
    return 1.0 / (1.0 + jnp.exp(-x))


def _dot(a, b, precision=None):
    return jnp.dot(a, b, precision=precision, preferred_element_type=F32)


def _split2(x):
    hi = x.astype(BF16)
    return hi, (x - hi.astype(F32)).astype(BF16)


def _dot_x3(x, w):
    xh, xl = _split2(x)
    wh, wl = _split2(w)
    return _dot(jnp.concatenate([xh, xl, xh], axis=1), jnp.concatenate([wh, wh, wl], axis=0))


def _seg_dot(x, m):
    xh, xl = _split2(x)
    mb = m.astype(BF16)
    return _dot(jnp.concatenate([xh, xl], axis=1), jnp.concatenate([mb, mb], axis=0))


def _block_diag_mean(width, seg):
    i = jnp.arange(width)
    return ((i[:, None] // seg) == (i[None, :] // seg)).astype(F32) / seg


def _mod_kernel(c_ref, w_ref, b_ref, o_ref):
    c = c_ref[...]
    s = c * _sigmoid(c)
    o_ref[0] = _dot(s, w_ref[0], HI) + b_ref[0]


def _modulations(c, c_ctx, mod_w, mod_b):
    depth, d, n = mod_w.shape
    nb = c.shape[0]
    rows = ((nb + 1 + 7) // 8) * 8
    cc = jnp.zeros((rows, d), F32).at[:nb].set(c).at[nb].set(c_ctx)
    tn = n // 4
    out = pl.pallas_call(
        _mod_kernel,
        grid=(depth, n // tn),
        in_specs=[pl.BlockSpec((rows, d), lambda l, j: (0, 0)),
                  pl.BlockSpec((1, d, tn), lambda l, j: (l, 0, j)),
                  pl.BlockSpec((1, 1, tn), lambda l, j: (l, 0, j))],
        out_specs=pl.BlockSpec((1, rows, tn), lambda l, j: (l, 0, j)),
        out_shape=jax.ShapeDtypeStruct((depth, rows, n), F32),
        compiler_params=_params(("parallel", "parallel")),
    )(cc, mod_w, mod_b.reshape(depth, 1, n))
    lat = out[:, :nb]
    ctx = jnp.broadcast_to(out[:, nb:nb + 1], lat.shape)
    return jnp.stack([lat, ctx], axis=2)


def _pick_mod(mod, is_ctx, idx):
    lo = idx * D_MODEL
    return jnp.where(is_ctx, mod[1:2, lo:lo + D_MODEL], mod[0:1, lo:lo + D_MODEL])


def _inproj_kernel(x_ref, mod_ref, g_ref, ws_ref, wq_ref, wkv_ref, wr_ref,
                   s5_ref, q_ref, kv_ref, rw_ref, *, tm, ctx_len):
    j = pl.program_id(1)
    x = x_ref[0]
    row = j * tm + lax.broadcasted_iota(jnp.int32, (tm, 1), 0)
    is_ctx = row < ctx_len
    mod = mod_ref[0]
    h = _rms(x) * g_ref[...]
    h = h * (1.0 + _pick_mod(mod, is_ctx, 1)) + _pick_mod(mod, is_ctx, 0)
    hb = h.astype(BF16)
    s5_ref[0] = _dot(hb, ws_ref[...])
    q_ref[0] = _dot(hb, wq_ref[...])
    kv_ref[0] = _dot(hb, wkv_ref[...])
    rw_ref[0] = _dot(hb, wr_ref[...])


def _inproj(x, mod, g, w_in, tm, ctx_len):
    nb, nt, d = x.shape
    wb = w_in.astype(BF16)
    ws = wb[:, :S5_W]
    wq = wb[:, S5_W:S5_W + ATT_W]
    wkv = wb[:, S5_W + ATT_W:S5_W + ATT_W + 2 * ATT_KV]
    wr = jnp.pad(wb[:, S5_W + ATT_W + 2 * ATT_KV:], ((0, 0), (0, RW_PAD - RW_IN)))
    full = lambda a: pl.BlockSpec(a.shape, lambda b, j: (0,) * a.ndim)
    tok = lambda w: pl.BlockSpec((1, tm, w), lambda b, j: (b, j, 0))
    return pl.pallas_call(
        functools.partial(_inproj_kernel, tm=tm, ctx_len=ctx_len),
        grid=(nb, nt // tm),
        in_specs=[tok(d), pl.BlockSpec((1, 2, N_MOD * d), lambda b, j: (b, 0, 0)),
                  full(g), full(ws), full(wq), full(wkv), full(wr)],
        out_specs=[tok(S5_W), tok(ATT_W), tok(2 * ATT_KV), tok(RW_PAD)],
        out_shape=[jax.ShapeDtypeStruct((nb, nt, w), F32) for w in (S5_W, ATT_W, 2 * ATT_KV, RW_PAD)],
        compiler_params=_params(("parallel", "parallel")),
    )(x, mod, g, ws, wq, wkv, wr)


def _s5_operators(a_re, a_im, log_step, b_re, b_im, c_re, c_im, d):
    T = S5_T
    lam = lax.complex(a_re.astype(F32), a_im.astype(F32))
    step = jnp.exp(log_step.astype(F32))[..., None]
    lam_bar = jnp.exp(lam * step)
    bmat = lax.complex(b_re.astype(F32), b_im.astype(F32))
    b_bar = ((lam_bar - 1.0) / lam)[..., None] * bmat
    cmat = lax.complex(c_re.astype(F32), c_im.astype(F32))
    n = jnp.arange(T + 1, dtype=F32)
    pw = jnp.exp((lam * step)[..., None] * n)
    cp = jnp.swapaxes(cmat, -1, -2)[:, :, :, None, :] * pw[..., None]
    bp = jnp.transpose(pw, (0, 1, 3, 2))[:, :, :, None, :] * jnp.transpose(b_bar, (0, 1, 3, 2))[:, :, None]
    G, P, GH = S5_G, S5_P, S5_GH
    W = T * GH

    def ri_rows(z):
        return jnp.concatenate([jnp.real(z), -jnp.imag(z)], axis=1)

    def ri_cols(z):
        return jnp.concatenate([jnp.real(z), jnp.imag(z)], axis=2)

    bt = jnp.stack([jnp.concatenate([jnp.real(jnp.swapaxes(b_bar[di], -1, -2)),
                                     jnp.imag(jnp.swapaxes(b_bar[di], -1, -2))], axis=-1)
                    for di in range(2)], axis=1)
    cpk0 = ri_rows(cp[0, :, :, :T].reshape(G, P, W))
    cpk1 = ri_rows(cp[1, :, :, :T][:, :, ::-1].reshape(G, P, W))
    cc = jnp.concatenate([ri_rows(cp[0, :, :, 1:].reshape(G, P, W)),
                          ri_rows(cp[1, :, :, 1:][:, :, ::-1].reshape(G, P, W))], axis=1)
    bc = jnp.concatenate([ri_cols(bp[0, :, :T][:, ::-1].reshape(G, W, P)),
                          ri_cols(bp[1, :, :T].reshape(G, W, P))], axis=2)
    lt = pw[..., T]
    lam_rows = jnp.stack([jnp.concatenate([jnp.real(lt[0]), jnp.real(lt[0])], -1),
                          jnp.concatenate([-jnp.imag(lt[0]), jnp.imag(lt[0])], -1),
                          jnp.concatenate([jnp.real(lt[1]), jnp.real(lt[1])], -1),
                          jnp.concatenate([-jnp.imag(lt[1]), jnp.imag(lt[1])], -1)], axis=1)
    dvec = jnp.tile(d.astype(F32).reshape(G, 1, GH), (1, 1, T))
    return bt, cpk0, cpk1, bc, cc, lam_rows, dvec


def _s5_kernel(u_ref, bt_ref, cp0_ref, cp1_ref, bc_ref, cc_ref, lam_ref, d_ref, y_ref,
               kt_ref, spf_ref, spb_ref, ef_ref, eb_ref, *, n_ctx, n_chunks, nb):
    T, GH, P2 = S5_T, S5_GH, 2 * S5_P
    W = T * GH
    krow0 = _dot(bt_ref[0, 0], cp0_ref[0], HI)
    krow1 = _dot(bt_ref[0, 1], cp1_ref[0], HI)
    lane = lax.broadcasted_iota(jnp.int32, (GH, W), 1)
    for s in range(T):
        f = krow0 if s == 0 else pltpu.roll(krow0, s * GH, axis=1)
        f = jnp.where(lane >= s * GH, f, 0.0)
        sh = (T - 1 - s) * GH
        b = krow1 if sh == 0 else pltpu.roll(krow1, W - sh, axis=1)
        b = jnp.where(lane < (s + 1) * GH, b, 0.0)
        kt_ref[s * GH:(s + 1) * GH, :] = f + b
    u = u_ref[0]
    y = u * d_ref[0] + _dot_x3(u, kt_ref[...])
    e = _dot_x3(u, bc_ref[0])
    ef_ref[...] = e[:, :P2]
    eb_ref[...] = e[:, P2:]
    lam = lam_ref[0]

    def run(order, e_ref, sp_ref, ra, rb):
        s = jnp.zeros((nb, P2), F32)
        for c in order:
            rows = pl.ds((c // S5_CB) * S5_CB * nb + c % S5_CB, nb, stride=S5_CB)
            sp_ref[rows, :] = s
            s = ra * s + rb * pltpu.roll(s, S5_P, axis=1) + e_ref[rows, :]

    run(list(range(n_chunks)), ef_ref, spf_ref, lam[0:1], lam[1:2])
    run(list(range(n_ctx - 1, -1, -1)) + list(range(n_chunks - 1, n_ctx - 1, -1)), eb_ref, spb_ref,
        lam[2:3], lam[3:4])
    sp = jnp.concatenate([spf_ref[...], spb_ref[...]], axis=1)
    y_ref[0] = y + _dot_x3(sp, cc_ref[0])


S5_GL = 128 // S5_GH


def _s5_in_kernel(z_ref, u_ref, *, nb):
    T, GH, CB = S5_T, S5_GH, S5_CB

    def one_batch(b, carry):
        rows = pl.ds(pl.multiple_of(b * CB, CB), CB)
        for t in range(T):
            zt = z_ref[b, pl.ds(t, CB, stride=T), :]
            for g in range(S5_GL):
                u_ref[g, rows, t * GH:(t + 1) * GH] = zt[:, g * GH:(g + 1) * GH]
        return carry

    lax.fori_loop(0, nb, one_batch, 0)


def _s5_out_kernel(y_ref, o_ref, row_ref, *, nb):
    T, GH, CB = S5_T, S5_GH, S5_CB

    def one_batch(b, carry):
        rows = pl.ds(pl.multiple_of(b * CB, CB), CB)
        for t in range(T):
            for g in range(S5_GL):
                row_ref[:, g * GH:(g + 1) * GH] = y_ref[g, rows, t * GH:(t + 1) * GH]
            o_ref[b, pl.ds(t, CB, stride=T), :] = row_ref[...]
        return carry

    lax.fori_loop(0, nb, one_batch, 0)


def _s5_scan(zs5, ops, ctx_len):
    nb, nt, _ = zs5.shape
    T, G, GH, CB, GL = S5_T, S5_G, S5_GH, S5_CB, S5_GL
    W = T * GH
    n_chunks = nt // T
    n_blocks = n_chunks // CB
    rows = n_chunks * nb
    relayout_grid = (n_blocks, G // GL)
    nat = pl.BlockSpec((nb, CB * T, GL * GH), lambda i, h: (0, i, h))
    chunked = pl.BlockSpec((GL, nb * CB, W), lambda i, h: (h, i, 0))
    u = pl.pallas_call(
        functools.partial(_s5_in_kernel, nb=nb),
        grid=relayout_grid,
        in_specs=[nat],
        out_specs=chunked,
        out_shape=jax.ShapeDtypeStruct((G, rows, W), F32),
        compiler_params=_params(("parallel", "parallel")),
    )(zs5)
    bt, cpk0, cpk1, bc, cc, lam_rows, dvec = ops
    per_g = lambda a: pl.BlockSpec((1,) + a.shape[1:], lambda g: (g,) + (0,) * (a.ndim - 1))
    y = pl.pallas_call(
        functools.partial(_s5_kernel, n_ctx=ctx_len // T, n_chunks=n_chunks, nb=nb),
        grid=(G,),
        in_specs=[per_g(u), per_g(bt), per_g(cpk0), per_g(cpk1), per_g(bc), per_g(cc),
                  per_g(lam_rows), per_g(dvec)],
        out_specs=per_g(u),
        out_shape=jax.ShapeDtypeStruct(u.shape, F32),
        scratch_shapes=[pltpu.VMEM((W, W), F32)] + [pltpu.VMEM((rows, 2 * S5_P), F32)] * 4,
        compiler_params=_params(("parallel",)),
    )(u, bt, cpk0, cpk1, bc, cc, lam_rows, dvec)
    return pl.pallas_call(
        functools.partial(_s5_out_kernel, nb=nb),
        grid=relayout_grid,
        in_specs=[chunked],
        out_specs=nat,
        out_shape=jax.ShapeDtypeStruct(zs5.shape, F32),
        scratch_shapes=[pltpu.VMEM((CB, GL * GH), F32)],
        compiler_params=_params(("parallel", "parallel")),
    )(y)


def _rope_tables(nt, ctx_len, width):
    tok = jnp.arange(nt - ctx_len, dtype=jnp.int32)
    pos_row = (tok // GRID_W).astype(F32)
    pos_col = (tok % GRID_W).astype(F32)
    inv = ROPE_THETA ** (-jnp.arange(ROPE_PAIRS, dtype=F32) / ROPE_PAIRS)
    ang_r = pos_row[:, None] * inv
    ang_c = pos_col[:, None] * inv
    ang = jnp.concatenate([ang_r, ang_r, ang_c, ang_c], axis=1)
    sign = jnp.tile(jnp.concatenate([-jnp.ones(ROPE_PAIRS, F32), jnp.ones(ROPE_PAIRS, F32)]), 2)
    cos = jnp.concatenate([jnp.ones((ctx_len, HEAD_DIM), F32), jnp.cos(ang)], axis=0)
    sin = jnp.concatenate([jnp.zeros((ctx_len, HEAD_DIM), F32), jnp.sin(ang) * sign], axis=0)
    reps = width // HEAD_DIM
    return jnp.tile(cos, (1, reps)), jnp.tile(sin, (1, reps))


def _head_norm_rope(t, pmat, g, cos, sin):
    width = t.shape[-1]
    ms = _seg_dot(t * t, pmat)
    tn = t * lax.rsqrt(ms + RMS_EPS) * g
    lane = lax.broadcasted_iota(jnp.int32, tn.shape, 1)
    first = (lane % (2 * ROPE_PAIRS)) < ROPE_PAIRS
    partner = jnp.where(first, pltpu.roll(tn, width - ROPE_PAIRS, axis=1), pltpu.roll(tn, ROPE_PAIRS, axis=1))
    return tn * cos + partner * sin


def _attn_kernel(q_ref, kv_ref, cq_ref, sq_ref, ck_ref, sk_ref, qg_ref, kg_ref, og_ref, pq_ref, pk_ref,
                 o_ref, kn_ref, vb_ref, *, tq, ctx_len, nk):
    j = pl.program_id(1)

    @pl.when(j == 0)
    def _():
        kv = kv_ref[0]
        kr = _head_norm_rope(kv[:, :ATT_KV], pk_ref[...], kg_ref[...], ck_ref[...], sk_ref[...])
        ones = jnp.ones((kv.shape[0], HEAD_DIM), BF16)
        for g in range(N_KV):
            kn_ref[g] = kr[:, g * HEAD_DIM:(g + 1) * HEAD_DIM].astype(BF16)
            vb_ref[g, :, :HEAD_DIM] = kv[:, ATT_KV + g * HEAD_DIM:ATT_KV + (g + 1) * HEAD_DIM].astype(BF16)
            vb_ref[g, :, HEAD_DIM:] = ones

    qr = _head_norm_rope(q_ref[0], pq_ref[...], qg_ref[...], cq_ref[...], sq_ref[...]) * (
        HEAD_DIM ** -0.5 * math.log2(math.e))

    def attend(n_keys):
        outs = []
        for h in range(N_Q):
            g = h // GQA_REP
            qh = qr[:, h * HEAD_DIM:(h + 1) * HEAD_DIM].astype(BF16)
            s = lax.dot_general(qh, kn_ref[g, :n_keys, :], (((1,), (1,)), ((), ())),
                                preferred_element_type=F32)
            p = jnp.exp2(s - jnp.max(s, axis=-1, keepdims=True))
            ov = _dot(p.astype(BF16), vb_ref[g, :n_keys, :])
            outs.append(ov[:, :HEAD_DIM] / ov[:, HEAD_DIM:])
        o = jnp.concatenate(outs, axis=1)
        o_ref[0] = _rms(o) * og_ref[...]

    @pl.when(j * tq < ctx_len)
    def _():
        attend(ctx_len)

    @pl.when(j * tq >= ctx_len)
    def _():
        attend(nk)


def _attention(zq, zkv, qn_g, kn_g, out_g, tq, ctx_len):
    nb, nt, _ = zq.shape
    cq, sq = _rope_tables(nt, ctx_len, ATT_W)
    ck, sk = cq[:, :ATT_KV], sq[:, :ATT_KV]
    qg = jnp.tile(qn_g, N_Q).reshape(1, ATT_W)
    kg = jnp.tile(kn_g, N_KV).reshape(1, ATT_KV)
    pq = _block_diag_mean(ATT_W, HEAD_DIM)
    pk = _block_diag_mean(ATT_KV, HEAD_DIM)
    full = lambda a: pl.BlockSpec(a.shape, lambda b, j: (0,) * a.ndim)
    return pl.pallas_call(
        functools.partial(_attn_kernel, tq=tq, ctx_len=ctx_len, nk=nt),
        grid=(nb, nt // tq),
        in_specs=[pl.BlockSpec((1, tq, ATT_W), lambda b, j: (b, j, 0)),
                  pl.BlockSpec((1, nt, 2 * ATT_KV), lambda b, j: (b, 0, 0)),
                  pl.BlockSpec((tq, ATT_W), lambda b, j: (j, 0)),
                  pl.BlockSpec((tq, ATT_W), lambda b, j: (j, 0)),
                  full(ck), full(sk), full(qg), full(kg),
                  pl.BlockSpec((1, ATT_W), lambda b, j: (0, 0)), full(pq), full(pk)],
        out_specs=pl.BlockSpec((1, tq, ATT_W), lambda b, j: (b, j, 0)),
        out_shape=jax.ShapeDtypeStruct((nb, nt, ATT_W), F32),
        scratch_shapes=[pltpu.VMEM((N_KV, nt, HEAD_DIM), BF16), pltpu.VMEM((N_KV, nt, 2 * HEAD_DIM), BF16)],
        compiler_params=_params(("parallel", "arbitrary")),
    )(zq, zkv, cq, sq, ck, sk, qg, kg, out_g.reshape(1, ATT_W), pq, pk)


def _rw_prep_kernel(z_ref, zp_ref, zn_ref, mu_ref, wl_ref, w0_ref, a0_ref, kk_ref, ka_ref, rk_ref, ones_ref,
                    r_ref, v_ref, a_ref, w_ref, b_ref, kd_ref, g_ref, bon_ref, *, tm, ctx_len, nt):
    j = pl.program_id(1)
    z = z_ref[0]
    row_l = lax.broadcasted_iota(jnp.int32, (tm, 1), 0)
    row = j * tm + row_l
    prev = jnp.where(row_l == 0, zp_ref[0, 7:8, :], pltpu.roll(z, 1, axis=0))
    prev = jnp.where((row == 0) | (row == ctx_len), 0.0, prev)
    nxt = jnp.where(row_l == tm - 1, zn_ref[0, 0:1, :], pltpu.roll(z, tm - 1, axis=0))
    nxt = jnp.where((row == ctx_len - 1) | (row == nt - 1), 0.0, nxt)
    zs = z + mu_ref[...] * (0.5 * (prev + nxt) - z)
    r = zs[:, 0:RW_W]
    k = zs[:, RW_W:2 * RW_W]
    v = zs[:, 2 * RW_W:3 * RW_W]
    lo = zs[:, 3 * RW_W:]
    lane = lax.broadcasted_iota(jnp.int32, lo.shape, 1)
    lo = jnp.where(lane < 2 * LORA_W, jnp.tanh(lo),
                   jnp.where(lane < 2 * LORA_W + 2 * LORA_A, lo, _sigmoid(lo)))
    proj = _dot_x3(lo, wl_ref[...])
    ones = ones_ref[...]
    kk = k * kk_ref[...]
    nrm = jnp.sqrt(_seg_dot(kk * kk, ones))
    kk = kk / jnp.maximum(nrm, 1e-12)
    r_ref[0] = r
    v_ref[0] = v
    a_ref[0] = -kk
    g_ref[0] = proj[:, 4 * RW_W:5 * RW_W]
    bon = jnp.zeros_like(r)
    for di in range(2):
        pre = w0_ref[di:di + 1, :] + proj[:, di * RW_W:(di + 1) * RW_W]
        w_log = -(jnp.maximum(-pre, 0.0) + jnp.log(1.0 + jnp.exp(-jnp.abs(pre)))) - 0.5
        w_ref[di, 0] = jnp.exp(-jnp.exp(w_log))
        iclr = _sigmoid(a0_ref[di:di + 1, :] + proj[:, (2 + di) * RW_W:(3 + di) * RW_W])
        kd = k * (1.0 + (iclr - 1.0) * ka_ref[...])
        kd_ref[di, 0] = kd
        b_ref[di, 0] = kk * iclr
        bon = bon + _seg_dot(r * kd * rk_ref[...], ones) * v
    bon_ref[0] = bon


def _rw_prep(zrw, mu, w0, w2, a0, a2, g2, k_k, k_a, r_k, tm, ctx_len):
    nb, nt, _ = zrw.shape
    wl = jnp.zeros((RW_PAD - 3 * RW_W, 5 * RW_W), F32)
    wl = wl.at[0:LORA_W, 0:RW_W].set(w2[0]).at[LORA_W:2 * LORA_W, RW_W:2 * RW_W].set(w2[1])
    o = 2 * LORA_W
    wl = wl.at[o:o + LORA_A, 2 * RW_W:3 * RW_W].set(a2[0]).at[o + LORA_A:o + 2 * LORA_A, 3 * RW_W:4 * RW_W].set(a2[1])
    o = 2 * LORA_W + 2 * LORA_A
    wl = wl.at[o:o + LORA_G, 4 * RW_W:5 * RW_W].set(g2)
    mu_p = jnp.pad(mu, (0, RW_PAD - RW_IN)).reshape(1, RW_PAD)
    ones = _block_diag_mean(RW_W, RW_HD) * RW_HD
    row = lambda a: a.reshape(1, RW_W)
    full = lambda a: pl.BlockSpec(a.shape, lambda b, j: (0,) * a.ndim)
    tok = pl.BlockSpec((1, tm, RW_W), lambda b, j: (b, j, 0))
    tok2 = pl.BlockSpec((2, 1, tm, RW_W), lambda b, j: (0, b, j, 0))
    one = jax.ShapeDtypeStruct((nb, nt, RW_W), F32)
    two = jax.ShapeDtypeStruct((2, nb, nt, RW_W), F32)
    n8 = nt // 8
    args = (zrw, zrw, zrw, mu_p, wl, w0, a0, row(k_k), row(k_a), row(r_k), ones)
    return pl.pallas_call(
        functools.partial(_rw_prep_kernel, tm=tm, ctx_len=ctx_len, nt=nt),
        grid=(nb, nt // tm),
        in_specs=[pl.BlockSpec((1, tm, RW_PAD), lambda b, j: (b, j, 0)),
                  pl.BlockSpec((1, 8, RW_PAD), lambda b, j: (b, jnp.maximum(j * (tm // 8) - 1, 0), 0)),
                  pl.BlockSpec((1, 8, RW_PAD), lambda b, j: (b, jnp.minimum((j + 1) * (tm // 8), n8 - 1), 0)),
                  ] + [full(a) for a in args[3:]],
        out_specs=[tok, tok, tok, tok2, tok2, tok2, tok, tok],
        out_shape=[one, one, one, two, two, two, one, one],
        compiler_params=_params(("parallel", "parallel")),
    )(*args)


RW_VH = RW_HD // 2
RW_SLOTS = 4


def _to_chains_kernel(*refs, nb, tt):
    *x_refs, o_ref, scr_ref = refs
    for j, ref in enumerate(x_refs):
        for b in range(nb):
            xt = ref[0, b].T
            for h in range(RW_H):
                scr_ref[(j * nb + b) * RW_H + h] = xt[h * RW_HD:(h + 1) * RW_HD, :]
    reps = RW_SLOTS // len(x_refs)
    for kg in range(RW_HD // 8):
        blk = jnp.swapaxes(scr_ref[:, kg * 8:(kg + 1) * 8, :], 0, 1)
        blk = jnp.concatenate([blk] * reps, axis=1)
        o_ref[:, kg * 8:(kg + 1) * 8, :] = jnp.swapaxes(jnp.swapaxes(blk, 1, 2), 0, 1)


def _to_chains(xs, tt):
    _, nb, nt, _ = xs[0][0].shape
    lanes = RW_SLOTS * nb * RW_H
    spec = lambda di: pl.BlockSpec((1, nb, tt, RW_W), lambda i: (di, 0, i, 0))
    out = pl.pallas_call(
        functools.partial(_to_chains_kernel, nb=nb, tt=tt),
        grid=(nt // tt,),
        in_specs=[spec(di) for _, di in xs],
        out_specs=pl.BlockSpec((tt, RW_HD, lanes), lambda i: (i, 0, 0)),
        out_shape=jax.ShapeDtypeStruct((nt, RW_HD, lanes), F32),
        scratch_shapes=[pltpu.VMEM((len(xs) * nb * RW_H, RW_HD, tt), F32)],
        compiler_params=_params(("parallel",)),
    )(*[x for x, _ in xs])
    return out.reshape(nt * RW_HD, lanes)


def _rw_scan_kernel(fa_ref, fc_ref, ba_ref, bc_ref, yf_ref, yb_ref, s_ref, vec_ref, *, ts, lanes):
    @pl.when(pl.program_id(0) == 0)
    def _():
        s_ref[...] = jnp.zeros_like(s_ref)

    lane = lax.broadcasted_iota(jnp.int32, (RW_HD, lanes), 1)
    q = lanes // 4
    even = (lane // q) % 2 == 0
    low = lane < 2 * q
    low_v = lax.broadcasted_iota(jnp.int32, (RW_VH, lanes), 1) < 2 * q

    def pairs(x):
        xs = pltpu.roll(x, 2 * q, axis=1)
        return jnp.where(low, x, xs), jnp.where(low, xs, x)

    def merge(f, b):
        return (jnp.where(even, f, pltpu.roll(b, q, axis=1)),
                jnp.where(even, pltpu.roll(f, lanes - q, axis=1), b))

    def prepare(i, slot):
        fr = pl.ds(pl.multiple_of(i * RW_HD, RW_HD), RW_HD)
        br = pl.ds(pl.multiple_of((ts - 1 - i) * RW_HD, RW_HD), RW_HD)
        f01, f23 = pairs(fa_ref[fr, :])
        b01, b23 = pairs(ba_ref[br, :])
        vec_ref[slot, 0], vec_ref[slot, 1] = merge(f01, b01)
        vec_ref[slot, 2], vec_ref[slot, 3] = merge(f23, b23)
        vec_ref[slot, 4], vk = merge(fc_ref[fr, :], bc_ref[br, :])
        vec_ref[slot, 5, :RW_VH] = jnp.where(low_v, vk[:RW_VH], vk[RW_VH:])

    def update(i, slot):
        for vi in range(RW_VH):
            s = s_ref[vi]
            sa = jnp.sum(s * vec_ref[slot, 3], axis=0, keepdims=True)
            s = s * vec_ref[slot, 0] + sa * vec_ref[slot, 1] + vec_ref[slot, 5, vi:vi + 1, :] * vec_ref[slot, 2]
            s_ref[vi] = s
            y = jnp.sum(s * vec_ref[slot, 4], axis=0, keepdims=True)
            yf_ref[pl.ds(i * RW_VH + vi, 1), :] = y
            yb_ref[pl.ds((ts - 1 - i) * RW_VH + vi, 1), :] = y

    prepare(0, 0)

    def two_steps(h, carry):
        i = 2 * h
        prepare(i + 1, 1)
        update(i, 0)
        prepare(jnp.minimum(i + 2, ts - 1), 0)
        update(i + 1, 1)
        return carry

    lax.fori_loop(0, ts // 2, two_steps, 0)


def _from_chains_kernel(yf_ref, yb_ref, of_ref, ob_ref, scr_ref, *, nb, tt):
    q = nb * RW_H
    for d, (y_ref, o_ref) in enumerate(((yf_ref, of_ref), (yb_ref, ob_ref))):
        for vg in range(RW_VH // 8):
            blk = jnp.swapaxes(y_ref[:, vg * 8:(vg + 1) * 8, :], 0, 1)
            blk = jnp.swapaxes(jnp.swapaxes(blk, 1, 2), 0, 1)
            for vh in range(2):
                src = (vh * 2 + d) * q
                scr_ref[:, vh * RW_VH + vg * 8:vh * RW_VH + (vg + 1) * 8, :] = blk[src:src + q]
        for b in range(nb):
            o_ref[b] = scr_ref[b * RW_H:(b + 1) * RW_H].reshape(RW_W, tt).T


def _rw_scan(r, v, a, w, b, kd, ts, tt, ctx_len):
    nb, nt, _ = r.shape
    lanes = RW_SLOTS * nb * RW_H
    one = lambda x: x.reshape(1, nb, nt, RW_W)
    r1, v1, a1 = one(r), one(v), one(a)
    fa = _to_chains([(w, 0), (b, 0), (kd, 0), (a1, 0)], tt)
    ba = _to_chains([(w, 1), (b, 1), (kd, 1), (a1, 0)], tt)
    fc = _to_chains([(r1, 0), (v1, 0)], tt)
    n_ctx = ctx_len // ts
    n_all = nt // ts
    fwd = lambda g: (g, 0)
    bwd = lambda g: (jnp.where(g < n_ctx, n_ctx - 1 - g, n_all - 1 - g + n_ctx), 0)
    kin = lambda m: pl.BlockSpec((ts * RW_HD, lanes), m)
    yout = lambda m: pl.BlockSpec((ts * RW_VH, lanes), m)
    yshape = jax.ShapeDtypeStruct((nt * RW_VH, lanes), F32)
    yf, yb = pl.pallas_call(
        functools.partial(_rw_scan_kernel, ts=ts, lanes=lanes),
        grid=(n_all,),
        in_specs=[kin(fwd), kin(fwd), kin(bwd), kin(bwd)],
        out_specs=[yout(fwd), yout(bwd)],
        out_shape=[yshape, yshape],
        scratch_shapes=[pltpu.VMEM((RW_VH, RW_HD, lanes), F32), pltpu.VMEM((2, 6, RW_HD, lanes), F32)],
        compiler_params=_params(("arbitrary",)),
    )(fa, fc, ba, fc)
    nat = jax.ShapeDtypeStruct((nb, nt, RW_W), F32)
    return pl.pallas_call(
        functools.partial(_from_chains_kernel, nb=nb, tt=tt),
        grid=(nt // tt,),
        in_specs=[pl.BlockSpec((tt, RW_VH, lanes), lambda i: (i, 0, 0))] * 2,
        out_specs=[pl.BlockSpec((nb, tt, RW_W), lambda i: (0, i, 0))] * 2,
        out_shape=[nat, nat],
        scratch_shapes=[pltpu.VMEM((nb * RW_H, RW_HD, tt), F32)],
        compiler_params=_params(("parallel",)),
    )(yf.reshape(nt, RW_VH, lanes), yb.reshape(nt, RW_VH, lanes))


def _outproj_kernel(x_ref, mod_ref, ys_ref, at_ref, wf_ref, wb_ref, bon_ref, g_ref,
                    gw_ref, gb_ref, sg_ref, lg_ref, lb_ref, avg_ref, wo_ref, o_ref, *, tm, ctx_len):
    j = pl.program_id(1)
    row = j * tm + lax.broadcasted_iota(jnp.int32, (tm, 1), 0)
    is_ctx = row < ctx_len
    a = jax.nn.gelu(ys_ref[0])
    o1 = _rms(a * _sigmoid(_dot_x3(a, gw_ref[...]) + gb_ref[...])) * sg_ref[...]
    wkv = wf_ref[0] + wb_ref[0]
    avg = avg_ref[...]
    cen = wkv - _seg_dot(wkv, avg)
    var = _seg_dot(cen * cen, avg)
    o3 = (cen * lax.rsqrt(var + RW_LN_EPS) * lg_ref[...] + lb_ref[...] + bon_ref[0]) * g_ref[0]
    o = (_dot(o1.astype(BF16), wo_ref[0:S5_W, :])
         + _dot(at_ref[0].astype(BF16), wo_ref[S5_W:S5_W + ATT_W, :])
         + _dot(o3.astype(BF16), wo_ref[S5_W + ATT_W:, :]))
    o_ref[0] = x_ref[0] + _pick_mod(mod_ref[0], is_ctx, 2) * o


def _outproj(x, mod, ys5, att, wkv_f, wkv_b, bonus, g, glu_w, glu_b, s5_out_g, ln_g, ln_b, w_out, tm, ctx_len):
    nb, nt, d = x.shape
    row = lambda a: a.reshape(1, -1)
    avg = _block_diag_mean(RW_W, RW_HD)
    params = (glu_w, row(glu_b), row(s5_out_g), row(ln_g), row(ln_b), avg, w_out.astype(BF16))
    full = lambda a: pl.BlockSpec(a.shape, lambda b, j: (0,) * a.ndim)
    tok = lambda w: pl.BlockSpec((1, tm, w), lambda b, j: (b, j, 0))
    return pl.pallas_call(
        functools.partial(_outproj_kernel, tm=tm, ctx_len=ctx_len),
        grid=(nb, nt // tm),
        in_specs=[tok(d), pl.BlockSpec((1, 2, N_MOD * d), lambda b, j: (b, 0, 0)),
                  tok(S5_W), tok(ATT_W), tok(RW_W), tok(RW_W), tok(RW_W), tok(RW_W)] + [full(a) for a in params],
        out_specs=tok(d),
        out_shape=jax.ShapeDtypeStruct(x.shape, F32),
        compiler_params=_params(("parallel", "parallel")),
    )(x, mod, ys5, att, wkv_f, wkv_b, bonus, g, *params)


FFN_HALO = 16
FFN_TF = 256


def _ffn_kernel(x_ref, xp_ref, xn_ref, mod_ref, g_ref, upg_ref, upv_ref, cwg_ref, cwv_ref, cbg_ref, cbv_ref,
                dn_ref, o_ref, hs_ref, acc_ref, *, tm, ctx_len, nt):
    j = pl.program_id(1)
    f = pl.program_id(2)
    hl, tf = FFN_HALO, FFN_TF
    mod = mod_ref[0]

    def adaln(xv, first_row):
        rows = first_row + lax.broadcasted_iota(jnp.int32, (xv.shape[0], 1), 0)
        is_ctx = rows < ctx_len
        h = _rms(xv) * g_ref[...]
        return (h * (1.0 + _pick_mod(mod, is_ctx, 4)) + _pick_mod(mod, is_ctx, 3)).astype(BF16)

    @pl.when(f == 0)
    def _():
        hs_ref[0:hl, :] = adaln(xp_ref[0], j * tm - hl)
        hs_ref[hl:hl + tm, :] = adaln(x_ref[0], j * tm)
        hs_ref[hl + tm:, :] = adaln(xn_ref[0], (j + 1) * tm)
        acc_ref[...] = jnp.zeros_like(acc_ref)

    n_ext = tm + 2 * hl
    row = j * tm + lax.broadcasted_iota(jnp.int32, (tm, 1), 0)
    no_prev = (row == 0) | (row == ctx_len)
    no_next = (row == ctx_len - 1) | (row == nt - 1)
    hs = hs_ref[...]

    def conv(up_ref, cw_ref, cb_ref):
        u = _dot(hs, up_ref[...])
        prev = jnp.where(no_prev, 0.0, pltpu.roll(u, 1, axis=0)[hl:hl + tm])
        nxt = jnp.where(no_next, 0.0, pltpu.roll(u, n_ext - 1, axis=0)[hl:hl + tm])
        cw = cw_ref[...]
        return cw[0:1] * prev + cw[1:2] * u[hl:hl + tm] + cw[2:3] * nxt + cb_ref[...]

    half = 0.5 * conv(upg_ref, cwg_ref, cbg_ref)
    act = (half + half * jnp.tanh(half)) * conv(upv_ref, cwv_ref, cbv_ref)
    acc_ref[...] += _dot(act.astype(BF16), dn_ref[...])

    @pl.when(f == pl.num_programs(2) - 1)
    def _():
        rows = j * tm + lax.broadcasted_iota(jnp.int32, (tm, 1), 0)
        o_ref[0] = x_ref[0] + _pick_mod(mod, rows < ctx_len, 5) * acc_ref[...]


def _conv_ffn(x, mod, g, up, conv_w, conv_b, down, tm, ctx_len):
    nb, nt, d = x.shape
    tf, hl = FFN_TF, FFN_HALO
    nf = D_FF // tf
    upb = up.astype(BF16)
    cb = conv_b.reshape(1, -1)
    nh = nt // hl
    cols = lambda rows: [pl.BlockSpec((rows, tf), lambda b, j, f: (0, f)),
                         pl.BlockSpec((rows, tf), lambda b, j, f: (0, nf + f))]
    return pl.pallas_call(
        functools.partial(_ffn_kernel, tm=tm, ctx_len=ctx_len, nt=nt),
        grid=(nb, nt // tm, nf),
        in_specs=[pl.BlockSpec((1, tm, d), lambda b, j, f: (b, j, 0)),
                  pl.BlockSpec((1, hl, d), lambda b, j, f: (b, jnp.maximum(j * (tm // hl) - 1, 0), 0)),
                  pl.BlockSpec((1, hl, d), lambda b, j, f: (b, jnp.minimum((j + 1) * (tm // hl), nh - 1), 0)),
                  pl.BlockSpec((1, 2, N_MOD * d), lambda b, j, f: (b, 0, 0)),
                  pl.BlockSpec((1, d), lambda b, j, f: (0, 0))] + cols(d) + cols(3) + cols(1) + [
                  pl.BlockSpec((tf, d), lambda b, j, f: (f, 0))],
        out_specs=pl.BlockSpec((1, tm, d), lambda b, j, f: (b, j, 0)),
        out_shape=jax.ShapeDtypeStruct(x.shape, F32),
        scratch_shapes=[pltpu.VMEM((tm + 2 * hl, d), BF16), pltpu.VMEM((tm, d), F32)],
        compiler_params=_params(("parallel", "parallel", "arbitrary")),
    )(x, x, x, mod, g.reshape(1, d), upb, upb, conv_w, conv_w, cb, cb, down.astype(BF16))


def _final_kernel(x_ref, g_ref, o_ref):
    o_ref[0] = _rms(x_ref[0]) * g_ref[...]


def _final_norm(x, g, ctx_len, tm):
    nb, nt, d = x.shape
    off = ctx_len // tm
    return pl.pallas_call(
        _final_kernel,
        grid=(nb, (nt - ctx_len) // tm),
        in_specs=[pl.BlockSpec((1, tm, d), lambda b, j: (b, j + off, 0)),
                  pl.BlockSpec((1, d), lambda b, j: (0, 0))],
        out_specs=pl.BlockSpec((1, tm, d), lambda b, j: (b, j, 0)),
        out_shape=jax.ShapeDtypeStruct((nb, nt - ctx_len, d), F32),
        compiler_params=_params(("parallel", "parallel")),
    )(x, g.reshape(1, d))


def _forward(p, *, tm, tq, ts, tt, tfin):
    x, ctx = p["x"], p["ctx"]
    ctx_len = ctx.shape[1]
    depth = p["mod_w"].shape[0]
    xa = jnp.concatenate([ctx, x], axis=1)
    mods = _modulations(p["c"], p["c_ctx"], p["mod_w"], p["mod_b"])
    for l in range(depth):
        mod = mods[l]
        zs5, zq, zkv, zrw = _inproj(xa, mod, p["norm1_g"][l].reshape(1, -1), p["w_in"][l], tm, ctx_len)
        ops = _s5_operators(p["s5_a_re"][l], p["s5_a_im"][l], p["s5_log_step"][l], p["s5_b_re"][l],
                            p["s5_b_im"][l], p["s5_c_re"][l], p["s5_c_im"][l], p["s5_d"][l])
        ys5 = _s5_scan(zs5, ops, ctx_len)
        att = _attention(zq, zkv, p["att_qn_g"][l], p["att_kn_g"][l], p["att_out_g"][l], tq, ctx_len)
        r, v, a, w, b, kd, g, bonus = _rw_prep(zrw, p["rw_mu"][l], p["rw_w0"][l], p["rw_w2"][l], p["rw_a0"][l],
                                                 p["rw_a2"][l], p["rw_g2"][l], p["rw_k_k"][l], p["rw_k_a"][l],
                                                 p["rw_r_k"][l], tm, ctx_len)
        wkv_f, wkv_b = _rw_scan(r, v, a, w, b, kd, ts, tt, ctx_len)
        xa = _outproj(xa, mod, ys5, att, wkv_f, wkv_b, bonus, g, p["s5_glu_w"][l], p["s5_glu_b"][l],
                      p["s5_out_g"][l], p["rw_ln_g"][l], p["rw_ln_b"][l], p["w_out"][l], tm, ctx_len)
        xa = _conv_ffn(xa, mod, p["norm2_g"][l], p["ffn_up"][l], p["ffn_conv_w"][l], p["ffn_conv_b"][l],
                       p["ffn_down"][l], tm, ctx_len)
    return _final_norm(xa, p["final_g"], ctx_len, tfin)


_ARG_NAMES = ("x c ctx c_ctx norm1_g norm2_g mod_w mod_b w_in w_out s5_a_re s5_a_im s5_log_step s5_b_re s5_b_im "
              "s5_c_re s5_c_im s5_d s5_glu_w s5_glu_b s5_out_g att_qn_g att_kn_g att_out_g rw_mu rw_w0 rw_w2 "
              "rw_a0 rw_a2 rw_g2 rw_k_k rw_k_a rw_r_k rw_ln_g rw_ln_b ffn_up ffn_conv_w ffn_conv_b ffn_down "
              "final_g").split()


def kernel(x, c, ctx, c_ctx, norm1_g, norm2_g, mod_w, mod_b, w_in, w_out, s5_a_re, s5_a_im, s5_log_step, s5_b_re, s5_b_im, s5_c_re, s5_c_im, s5_d, s5_glu_w, s5_glu_b, s5_out_g, att_qn_g, att_kn_g, att_out_g, rw_mu, rw_w0, rw_w2, rw_a0, rw_a2, rw_g2, rw_k_k, rw_k_a, rw_r_k, rw_ln_g, rw_ln_b, ffn_up, ffn_conv_w, ffn_conv_b, ffn_down, final_g):
    args = (x, c, ctx, c_ctx, norm1_g, norm2_g, mod_w, mod_b, w_in, w_out, s5_a_re, s5_a_im, s5_log_step, s5_b_re,
            s5_b_im, s5_c_re, s5_c_im, s5_d, s5_glu_w, s5_glu_b, s5_out_g, att_qn_g, att_kn_g, att_out_g, rw_mu,
            rw_w0, rw_w2, rw_a0, rw_a2, rw_g2, rw_k_k, rw_k_a, rw_r_k, rw_ln_g, rw_ln_b, ffn_up, ffn_conv_w,
            ffn_conv_b, ffn_down, final_g)
    return _forward(dict(zip(_ARG_NAMES, args)), tm=768, tq=256, ts=32, tt=128, tfin=256)
```

```python
import functools
import math

import jax
import jax.numpy as jnp
from jax import lax
from jax.experimental import pallas as pl
from jax.experimental.pallas import tpu as pltpu

F32 = jnp.float32
BF16 = jnp.bfloat16
HI = lax.Precision.HIGHEST

D_MODEL = 1024
GRID_W = 64
S5_W = 256
S5_GH = 16
S5_G = 16
S5_P = 64
HEAD_DIM = 64
ATT_W = 512
N_Q = 8
GQA_REP = 4
N_KV = 2
ATT_KV = 128
ROPE_PAIRS = 16
ROPE_THETA = 10000.0
RW_W = 256
RW_HD = 64
RW_H = 4
LORA_W = 32
LORA_A = 32
LORA_G = 64
RW_IN = 960
RW_PAD = 1024
RW_LN_EPS = 64e-5
D_FF = 2816
N_MOD = 6
RMS_EPS = 1e-6

S5_T = 32
S5_CB = 8
V7X_VMEM_LIMIT = 56 * 1024 * 1024


def _params(sem, vmem=V7X_VMEM_LIMIT):
    return pltpu.CompilerParams(dimension_semantics=sem, vmem_limit_bytes=vmem)


def _rms(x):
    return x * lax.rsqrt(jnp.mean(x * x, axis=-1, keepdims=True) + RMS_EPS)


def _sigmoid(x):
    return 1.0 / (1.0 + jnp.exp(-x))


def _dot(a, b, precision=None):
    return jnp.dot(a, b, precision=precision, preferred_element_type=F32)


def _split2(x):
    hi = x.astype(BF16)
    return hi, (x - hi.astype(F32)).astype(BF16)


def _dot_x3(x, w):
    xh, xl = _split2(x)
    wh, wl = _split2(w)
    return _dot(jnp.concatenate([xh, xl, xh], axis=1), jnp.concatenate([wh, wh, wl], axis=0))


def _seg_dot(x, m):
    xh, xl = _split2(x)
    mb = m.astype(BF16)
    return _dot(jnp.concatenate([xh, xl], axis=1), jnp.concatenate([mb, mb], axis=0))


def _block_diag_mean(width, seg):
    i = jnp.arange(width)
    return ((i[:, None] // seg) == (i[None, :] // seg)).astype(F32) / seg


def _mod_kernel(c_ref, w_ref, b_ref, o_ref):
    c = c_ref[...]
    s = c * _sigmoid(c)
    o_ref[0] = _dot(s, w_ref[0], HI) + b_ref[0]


def _modulations(c, c_ctx, mod_w, mod_b):
    depth, d, n = mod_w.shape
    nb = c.shape[0]
    rows = ((nb + 1 + 7) // 8) * 8
    cc = jnp.zeros((rows, d), F32).at[:nb].set(c).at[nb].set(c_ctx)
    tn = n // 4
    out = pl.pallas_call(
        _mod_kernel,
        grid=(depth, n // tn),
        in_specs=[pl.BlockSpec((rows, d), lambda l, j: (0, 0)),
                  pl.BlockSpec((1, d, tn), lambda l, j: (l, 0, j)),
                  pl.BlockSpec((1, 1, tn), lambda l, j: (l, 0, j))],
        out_specs=pl.BlockSpec((1, rows, tn), lambda l, j: (l, 0, j)),
        out_shape=jax.ShapeDtypeStruct((depth, rows, n), F32),
        compiler_params=_params(("parallel", "parallel")),
    )(cc, mod_w, mod_b.reshape(depth, 1, n))
    lat = out[:, :nb]
    ctx = jnp.broadcast_to(out[:, nb:nb + 1], lat.shape)
    return jnp.stack([lat, ctx], axis=2)


def _pick_mod(mod, is_ctx, idx):
    lo = idx * D_MODEL
    return jnp.where(is_ctx, mod[1:2, lo:lo + D_MODEL], mod[0:1, lo:lo + D_MODEL])


def _inproj_kernel(x_ref, mod_ref, g_ref, ws_ref, wq_ref, wkv_ref, wr_ref,
                   s5_ref, q_ref, kv_ref, rw_ref, *, tm, ctx_len):
    j = pl.program_id(1)
    x = x_ref[0]
    row = j * tm + lax.broadcasted_iota(jnp.int32, (tm, 1), 0)
    is_ctx = row < ctx_len
    mod = mod_ref[0]
    h = _rms(x) * g_ref[...]
    h = h * (1.0 + _pick_mod(mod, is_ctx, 1)) + _pick_mod(mod, is_ctx, 0)
    hb = h.astype(BF16)
    s5_ref[0] = _dot(hb, ws_ref[...])
    q_ref[0] = _dot(hb, wq_ref[...])
    kv_ref[0] = _dot(hb, wkv_ref[...])
    rw_ref[0] = _dot(hb, wr_ref[...])


def _inproj(x, mod, g, w_in, tm, ctx_len):
    nb, nt, d = x.shape
    wb = w_in.astype(BF16)
    ws = wb[:, :S5_W]
    wq = wb[:, S5_W:S5_W + ATT_W]
    wkv = wb[:, S5_W + ATT_W:S5_W + ATT_W + 2 * ATT_KV]
    wr = jnp.pad(wb[:, S5_W + ATT_W + 2 * ATT_KV:], ((0, 0), (0, RW_PAD - RW_IN)))
    full = lambda a: pl.BlockSpec(a.shape, lambda b, j: (0,) * a.ndim)
    tok = lambda w: pl.BlockSpec((1, tm, w), lambda b, j: (b, j, 0))
    return pl.pallas_call(
        functools.partial(_inproj_kernel, tm=tm, ctx_len=ctx_len),
        grid=(nb, nt // tm),
        in_specs=[tok(d), pl.BlockSpec((1, 2, N_MOD * d), lambda b, j: (b, 0, 0)),
                  full(g), full(ws), full(wq), full(wkv), full(wr)],
        out_specs=[tok(S5_W), tok(ATT_W), tok(2 * ATT_KV), tok(RW_PAD)],
        out_shape=[jax.ShapeDtypeStruct((nb, nt, w), F32) for w in (S5_W, ATT_W, 2 * ATT_KV, RW_PAD)],
        compiler_params=_params(("parallel", "parallel")),
    )(x, mod, g, ws, wq, wkv, wr)


def _s5_operators(a_re, a_im, log_step, b_re, b_im, c_re, c_im, d):
    T = S5_T
    lam = lax.complex(a_re.astype(F32), a_im.astype(F32))
    step = jnp.exp(log_step.astype(F32))[..., None]
    lam_bar = jnp.exp(lam * step)
    bmat = lax.complex(b_re.astype(F32), b_im.astype(F32))
    b_bar = ((lam_bar - 1.0) / lam)[..., None] * bmat
    cmat = lax.complex(c_re.astype(F32), c_im.astype(F32))
    n = jnp.arange(T + 1, dtype=F32)
    pw = jnp.exp((lam * step)[..., None] * n)
    cp = jnp.swapaxes(cmat, -1, -2)[:, :, :, None, :] * pw[..., None]
    bp = jnp.transpose(pw, (0, 1, 3, 2))[:, :, :, None, :] * jnp.transpose(b_bar, (0, 1, 3, 2))[:, :, None]
    G, P, GH = S5_G, S5_P, S5_GH
    W = T * GH

    def ri_rows(z):
        return jnp.concatenate([jnp.real(z), -jnp.imag(z)], axis=1)

    def ri_cols(z):
        return jnp.concatenate([jnp.real(z), jnp.imag(z)], axis=2)

    bt = jnp.stack([jnp.concatenate([jnp.real(jnp.swapaxes(b_bar[di], -1, -2)),
                                     jnp.imag(jnp.swapaxes(b_bar[di], -1, -2))], axis=-1)
                    for di in range(2)], axis=1)
    cpk0 = ri_rows(cp[0, :, :, :T].reshape(G, P, W))
    cpk1 = ri_rows(cp[1, :, :, :T][:, :, ::-1].reshape(G, P, W))
    cc = jnp.concatenate([ri_rows(cp[0, :, :, 1:].reshape(G, P, W)),
                          ri_rows(cp[1, :, :, 1:][:, :, ::-1].reshape(G, P, W))], axis=1)
    bc = jnp.concatenate([ri_cols(bp[0, :, :T][:, ::-1].reshape(G, W, P)),
                          ri_cols(bp[1, :, :T].reshape(G, W, P))], axis=2)
    lt = pw[..., T]
    lam_rows = jnp.stack([jnp.concatenate([jnp.real(lt[0]), jnp.real(lt[0])], -1),
                          jnp.concatenate([-jnp.imag(lt[0]), jnp.imag(lt[0])], -1),
                          jnp.concatenate([jnp.real(lt[1]), jnp.real(lt[1])], -1),
                          jnp.concatenate([-jnp.imag(lt[1]), jnp.imag(lt[1])], -1)], axis=1)
    dvec = jnp.tile(d.astype(F32).reshape(G, 1, GH), (1, 1, T))
    return bt, cpk0, cpk1, bc, cc, lam_rows, dvec


def _s5_kernel(u_ref, bt_ref, cp0_ref, cp1_ref, bc_ref, cc_ref, lam_ref, d_ref, y_ref,
               kt_ref, spf_ref, spb_ref, ef_ref, eb_ref, *, n_ctx, n_chunks, nb):
    T, GH, P2 = S5_T, S5_GH, 2 * S5_P
    W = T * GH
    krow0 = _dot(bt_ref[0, 0], cp0_ref[0], HI)
    krow1 = _dot(bt_ref[0, 1], cp1_ref[0], HI)
    lane = lax.broadcasted_iota(jnp.int32, (GH, W), 1)
    for s in range(T):
        f = krow0 if s == 0 else pltpu.roll(krow0, s * GH, axis=1)
        f = jnp.where(lane >= s * GH, f, 0.0)
        sh = (T - 1 - s) * GH
        b = krow1 if sh == 0 else pltpu.roll(krow1, W - sh, axis=1)
        b = jnp.where(lane < (s + 1) * GH, b, 0.0)
        kt_ref[s * GH:(s + 1) * GH, :] = f + b
    u = u_ref[0]
    y = u * d_ref[0] + _dot_x3(u, kt_ref[...])
    e = _dot_x3(u, bc_ref[0])
    ef_ref[...] = e[:, :P2]
    eb_ref[...] = e[:, P2:]
    lam = lam_ref[0]

    def run(order, e_ref, sp_ref, ra, rb):
        s = jnp.zeros((nb, P2), F32)
        for c in order:
            rows = pl.ds((c // S5_CB) * S5_CB * nb + c % S5_CB, nb, stride=S5_CB)
            sp_ref[rows, :] = s
            s = ra * s + rb * pltpu.roll(s, S5_P, axis=1) + e_ref[rows, :]

    run(list(range(n_chunks)), ef_ref, spf_ref, lam[0:1], lam[1:2])
    run(list(range(n_ctx - 1, -1, -1)) + list(range(n_chunks - 1, n_ctx - 1, -1)), eb_ref, spb_ref,
        lam[2:3], lam[3:4])
    sp = jnp.concatenate([spf_ref[...], spb_ref[...]], axis=1)
    y_ref[0] = y + _dot_x3(sp, cc_ref[0])


S5_GL = 128 // S5_GH


def _s5_in_kernel(z_ref, u_ref, *, nb):
    T, GH, CB = S5_T, S5_GH, S5_CB

    def one_batch(b, carry):
        rows = pl.ds(pl.multiple_of(b * CB, CB), CB)
        for t in range(T):
            zt = z_ref[b, pl.ds(t, CB, stride=T), :]
            for g in range(S5_GL):
                u_ref[g, rows, t * GH:(t + 1) * GH] = zt[:, g * GH:(g + 1) * GH]
        return carry

    lax.fori_loop(0, nb, one_batch, 0)


def _s5_out_kernel(y_ref, o_ref, row_ref, *, nb):
    T, GH, CB = S5_T, S5_GH, S5_CB

    def one_batch(b, carry):
        rows = pl.ds(pl.multiple_of(b * CB, CB), CB)
        for t in range(T):
            for g in range(S5_GL):
                row_ref[:, g * GH:(g + 1) * GH] = y_ref[g, rows, t * GH:(t + 1) * GH]
            o_ref[b, pl.ds(t, CB, stride=T), :] = row_ref[...]
        return carry

    lax.fori_loop(0, nb, one_batch, 0)


def _s5_scan(zs5, ops, ctx_len):
    nb, nt, _ = zs5.shape
    T, G, GH, CB, GL = S5_T, S5_G, S5_GH, S5_CB, S5_GL
    W = T * GH
    n_chunks = nt // T
    n_blocks = n_chunks // CB
    rows = n_chunks * nb
    relayout_grid = (n_blocks, G // GL)
    nat = pl.BlockSpec((nb, CB * T, GL * GH), lambda i, h: (0, i, h))
    chunked = pl.BlockSpec((GL, nb * CB, W), lambda i, h: (h, i, 0))
    u = pl.pallas_call(
        functools.partial(_s5_in_kernel, nb=nb),
        grid=relayout_grid,
        in_specs=[nat],
        out_specs=chunked,
        out_shape=jax.ShapeDtypeStruct((G, rows, W), F32),
        compiler_params=_params(("parallel", "parallel")),
    )(zs5)
    bt, cpk0, cpk1, bc, cc, lam_rows, dvec = ops
    per_g = lambda a: pl.BlockSpec((1,) + a.shape[1:], lambda g: (g,) + (0,) * (a.ndim - 1))
    y = pl.pallas_call(
        functools.partial(_s5_kernel, n_ctx=ctx_len // T, n_chunks=n_chunks, nb=nb),
        grid=(G,),
        in_specs=[per_g(u), per_g(bt), per_g(cpk0), per_g(cpk1), per_g(bc), per_g(cc),
                  per_g(lam_rows), per_g(dvec)],
        out_specs=per_g(u),
        out_shape=jax.ShapeDtypeStruct(u.shape, F32),
        scratch_shapes=[pltpu.VMEM((W, W), F32)] + [pltpu.VMEM((rows, 2 * S5_P), F32)] * 4,
        compiler_params=_params(("parallel",)),
    )(u, bt, cpk0, cpk1, bc, cc, lam_rows, dvec)
    return pl.pallas_call(
        functools.partial(_s5_out_kernel, nb=nb),
        grid=relayout_grid,
        in_specs=[chunked],
        out_specs=nat,
        out_shape=jax.ShapeDtypeStruct(zs5.shape, F32),
        scratch_shapes=[pltpu.VMEM((CB, GL * GH), F32)],
        compiler_params=_params(("parallel", "parallel")),
    )(y)


def _rope_tables(nt, ctx_len, width):
    tok = jnp.arange(nt - ctx_len, dtype=jnp.int32)
    pos_row = (tok // GRID_W).astype(F32)
    pos_col = (tok % GRID_W).astype(F32)
    inv = ROPE_THETA ** (-jnp.arange(ROPE_PAIRS, dtype=F32) / ROPE_PAIRS)
    ang_r = pos_row[:, None] * inv
    ang_c = pos_col[:, None] * inv
    ang = jnp.concatenate([ang_r, ang_r, ang_c, ang_c], axis=1)
    sign = jnp.tile(jnp.concatenate([-jnp.ones(ROPE_PAIRS, F32), jnp.ones(ROPE_PAIRS, F32)]), 2)
    cos = jnp.concatenate([jnp.ones((ctx_len, HEAD_DIM), F32), jnp.cos(ang)], axis=0)
    sin = jnp.concatenate([jnp.zeros((ctx_len, HEAD_DIM), F32), jnp.sin(ang) * sign], axis=0)
    reps = width // HEAD_DIM
    return jnp.tile(cos, (1, reps)), jnp.tile(sin, (1, reps))


def _head_norm_rope(t, pmat, g, cos, sin):
    width = t.shape[-1]
    ms = _seg_dot(t * t, pmat)
    tn = t * lax.rsqrt(ms + RMS_EPS) * g
    lane = lax.broadcasted_iota(jnp.int32, tn.shape, 1)
    first = (lane % (2 * ROPE_PAIRS)) < ROPE_PAIRS
    partner = jnp.where(first, pltpu.roll(tn, width - ROPE_PAIRS, axis=1), pltpu.roll(tn, ROPE_PAIRS, axis=1))
    return tn * cos + partner * sin


def _attn_kernel(q_ref, kv_ref, cq_ref, sq_ref, ck_ref, sk_ref, qg_ref, kg_ref, og_ref, pq_ref, pk_ref,
                 o_ref, kn_ref, vb_ref, *, tq, ctx_len, nk):
    j = pl.program_id(1)

    @pl.when(j == 0)
    def _():
        kv = kv_ref[0]
        kr = _head_norm_rope(kv[:, :ATT_KV], pk_ref[...], kg_ref[...], ck_ref[...], sk_ref[...])
        ones = jnp.ones((kv.shape[0], HEAD_DIM), BF16)
        for g in range(N_KV):
            kn_ref[g] = kr[:, g * HEAD_DIM:(g + 1) * HEAD_DIM].astype(BF16)
            vb_ref[g, :, :HEAD_DIM] = kv[:, ATT_KV + g * HEAD_DIM:ATT_KV + (g + 1) * HEAD_DIM].astype(BF16)
            vb_ref[g, :, HEAD_DIM:] = ones

    qr = _head_norm_rope(q_ref[0], pq_ref[...], qg_ref[...], cq_ref[...], sq_ref[...]) * (
        HEAD_DIM ** -0.5 * math.log2(math.e))

    def attend(n_keys):
        outs = []
        for h in range(N_Q):
            g = h // GQA_REP
            qh = qr[:, h * HEAD_DIM:(h + 1) * HEAD_DIM].astype(BF16)
            s = lax.dot_general(qh, kn_ref[g, :n_keys, :], (((1,), (1,)), ((), ())),
                                preferred_element_type=F32)
            p = jnp.exp2(s - jnp.max(s, axis=-1, keepdims=True))
            ov = _dot(p.astype(BF16), vb_ref[g, :n_keys, :])
            outs.append(ov[:, :HEAD_DIM] / ov[:, HEAD_DIM:])
        o = jnp.concatenate(outs, axis=1)
        o_ref[0] = _rms(o) * og_ref[...]

    @pl.when(j * tq < ctx_len)
    def _():
        attend(ctx_len)

    @pl.when(j * tq >= ctx_len)
    def _():
        attend(nk)


def _attention(zq, zkv, qn_g, kn_g, out_g, tq, ctx_len):
    nb, nt, _ = zq.shape
    cq, sq = _rope_tables(nt, ctx_len, ATT_W)
    ck, sk = cq[:, :ATT_KV], sq[:, :ATT_KV]
    qg = jnp.tile(qn_g, N_Q).reshape(1, ATT_W)
    kg = jnp.tile(kn_g, N_KV).reshape(1, ATT_KV)
    pq = _block_diag_mean(ATT_W, HEAD_DIM)
    pk = _block_diag_mean(ATT_KV, HEAD_DIM)
    full = lambda a: pl.BlockSpec(a.shape, lambda b, j: (0,) * a.ndim)
    return pl.pallas_call(
        functools.partial(_attn_kernel, tq=tq, ctx_len=ctx_len, nk=nt),
        grid=(nb, nt // tq),
        in_specs=[pl.BlockSpec((1, tq, ATT_W), lambda b, j: (b, j, 0)),
                  pl.BlockSpec((1, nt, 2 * ATT_KV), lambda b, j: (b, 0, 0)),
                  pl.BlockSpec((tq, ATT_W), lambda b, j: (j, 0)),
                  pl.BlockSpec((tq, ATT_W), lambda b, j: (j, 0)),
                  full(ck), full(sk), full(qg), full(kg),
                  pl.BlockSpec((1, ATT_W), lambda b, j: (0, 0)), full(pq), full(pk)],
        out_specs=pl.BlockSpec((1, tq, ATT_W), lambda b, j: (b, j, 0)),
        out_shape=jax.ShapeDtypeStruct((nb, nt, ATT_W), F32),
        scratch_shapes=[pltpu.VMEM((N_KV, nt, HEAD_DIM), BF16), pltpu.VMEM((N_KV, nt, 2 * HEAD_DIM), BF16)],
        compiler_params=_params(("parallel", "arbitrary")),
    )(zq, zkv, cq, sq, ck, sk, qg, kg, out_g.reshape(1, ATT_W), pq, pk)


def _rw_prep_kernel(z_ref, zp_ref, zn_ref, mu_ref, wl_ref, w0_ref, a0_ref, kk_ref, ka_ref, rk_ref, ones_ref,
                    r_ref, v_ref, a_ref, w_ref, b_ref, kd_ref, g_ref, bon_ref, *, tm, ctx_len, nt):
    j = pl.program_id(1)
    z = z_ref[0]
    row_l = lax.broadcasted_iota(jnp.int32, (tm, 1), 0)
    row = j * tm + row_l
    prev = jnp.where(row_l == 0, zp_ref[0, 7:8, :], pltpu.roll(z, 1, axis=0))
    prev = jnp.where((row == 0) | (row == ctx_len), 0.0, prev)
    nxt = jnp.where(row_l == tm - 1, zn_ref[0, 0:1, :], pltpu.roll(z, tm - 1, axis=0))
    nxt = jnp.where((row == ctx_len - 1) | (row == nt - 1), 0.0, nxt)
    zs = z + mu_ref[...] * (0.5 * (prev + nxt) - z)
    r = zs[:, 0:RW_W]
    k = zs[:, RW_W:2 * RW_W]
    v = zs[:, 2 * RW_W:3 * RW_W]
    lo = zs[:, 3 * RW_W:]
    lane = lax.broadcasted_iota(jnp.int32, lo.shape, 1)
    lo = jnp.where(lane < 2 * LORA_W, jnp.tanh(lo),
                   jnp.where(lane < 2 * LORA_W + 2 * LORA_A, lo, _sigmoid(lo)))
    proj = _dot_x3(lo, wl_ref[...])
    ones = ones_ref[...]
    kk = k * kk_ref[...]
    nrm = jnp.sqrt(_seg_dot(kk * kk, ones))
    kk = kk / jnp.maximum(nrm, 1e-12)
    r_ref[0] = r
    v_ref[0] = v
    a_ref[0] = -kk
    g_ref[0] = proj[:, 4 * RW_W:5 * RW_W]
    bon = jnp.zeros_like(r)
    for di in range(2):
        pre = w0_ref[di:di + 1, :] + proj[:, di * RW_W:(di + 1) * RW_W]
        w_log = -(jnp.maximum(-pre, 0.0) + jnp.log(1.0 + jnp.exp(-jnp.abs(pre)))) - 0.5
        w_ref[di, 0] = jnp.exp(-jnp.exp(w_log))
        iclr = _sigmoid(a0_ref[di:di + 1, :] + proj[:, (2 + di) * RW_W:(3 + di) * RW_W])
        kd = k * (1.0 + (iclr - 1.0) * ka_ref[...])
        kd_ref[di, 0] = kd
        b_ref[di, 0] = kk * iclr
        bon = bon + _seg_dot(r * kd * rk_ref[...], ones) * v
    bon_ref[0] = bon


def _rw_prep(zrw, mu, w0, w2, a0, a2, g2, k_k, k_a, r_k, tm, ctx_len):
    nb, nt, _ = zrw.shape
    wl = jnp.zeros((RW_PAD - 3 * RW_W, 5 * RW_W), F32)
    wl = wl.at[0:LORA_W, 0:RW_W].set(w2[0]).at[LORA_W:2 * LORA_W, RW_W:2 * RW_W].set(w2[1])
    o = 2 * LORA_W
    wl = wl.at[o:o + LORA_A, 2 * RW_W:3 * RW_W].set(a2[0]).at[o + LORA_A:o + 2 * LORA_A, 3 * RW_W:4 * RW_W].set(a2[1])
    o = 2 * LORA_W + 2 * LORA_A
    wl = wl.at[o:o + LORA_G, 4 * RW_W:5 * RW_W].set(g2)
    mu_p = jnp.pad(mu, (0, RW_PAD - RW_IN)).reshape(1, RW_PAD)
    ones = _block_diag_mean(RW_W, RW_HD) * RW_HD
    row = lambda a: a.reshape(1, RW_W)
    full = lambda a: pl.BlockSpec(a.shape, lambda b, j: (0,) * a.ndim)
    tok = pl.BlockSpec((1, tm, RW_W), lambda b, j: (b, j, 0))
    tok2 = pl.BlockSpec((2, 1, tm, RW_W), lambda b, j: (0, b, j, 0))
    one = jax.ShapeDtypeStruct((nb, nt, RW_W), F32)
    two = jax.ShapeDtypeStruct((2, nb, nt, RW_W), F32)
    n8 = nt // 8
    args = (zrw, zrw, zrw, mu_p, wl, w0, a0, row(k_k), row(k_a), row(r_k), ones)
    return pl.pallas_call(
        functools.partial(_rw_prep_kernel, tm=tm, ctx_len=ctx_len, nt=nt),
        grid=(nb, nt // tm),
        in_specs=[pl.BlockSpec((1, tm, RW_PAD), lambda b, j: (b, j, 0)),
                  pl.BlockSpec((1, 8, RW_PAD), lambda b, j: (b, jnp.maximum(j * (tm // 8) - 1, 0), 0)),
                  pl.BlockSpec((1, 8, RW_PAD), lambda b, j: (b, jnp.minimum((j + 1) * (tm // 8), n8 - 1), 0)),
                  ] + [full(a) for a in args[3:]],
        out_specs=[tok, tok, tok, tok2, tok2, tok2, tok, tok],
        out_shape=[one, one, one, two, two, two, one, one],
        compiler_params=_params(("parallel", "parallel")),
    )(*args)


RW_VH = RW_HD // 2
RW_SLOTS = 4
RW_VGROUP = 16


def _to_chains_kernel(*refs, nb, tt):
    *x_refs, o_ref, scr_ref = refs
    for j, ref in enumerate(x_refs):
        for b in range(nb):
            xt = ref[0, b].T
            for h in range(RW_H):
                scr_ref[(j * nb + b) * RW_H + h] = xt[h * RW_HD:(h + 1) * RW_HD, :]
    reps = RW_SLOTS // len(x_refs)
    for kg in range(RW_HD // 8):
        blk = jnp.swapaxes(scr_ref[:, kg * 8:(kg + 1) * 8, :], 0, 1)
        blk = jnp.concatenate([blk] * reps, axis=1)
        o_ref[:, kg * 8:(kg + 1) * 8, :] = jnp.swapaxes(jnp.swapaxes(blk, 1, 2), 0, 1)


def _to_chains(xs, tt):
    _, nb, nt, _ = xs[0][0].shape
    lanes = RW_SLOTS * nb * RW_H
    spec = lambda di: pl.BlockSpec((1, nb, tt, RW_W), lambda i: (di, 0, i, 0))
    out = pl.pallas_call(
        functools.partial(_to_chains_kernel, nb=nb, tt=tt),
        grid=(nt // tt,),
        in_specs=[spec(di) for _, di in xs],
        out_specs=pl.BlockSpec((tt, RW_HD, lanes), lambda i: (i, 0, 0)),
        out_shape=jax.ShapeDtypeStruct((nt, RW_HD, lanes), F32),
        scratch_shapes=[pltpu.VMEM((len(xs) * nb * RW_H, RW_HD, tt), F32)],
        compiler_params=_params(("parallel",)),
    )(*[x for x, _ in xs])
    return out.reshape(nt * RW_HD, lanes)


def _rw_scan_kernel(fa_ref, fc_ref, ba_ref, bc_ref, yf_ref, yb_ref, s_ref, vec_ref, *, ts, lanes):
    ones = jnp.ones((RW_HD, lanes), F32)

    @pl.when(pl.program_id(0) == 0)
    def _():
        s_ref[...] = jnp.zeros_like(s_ref)
        vec_ref[1, 5] = ones

    p_end = vec_ref[1, 5]
    for vi in range(RW_VH):
        s_ref[vi] = s_ref[vi] * p_end
    vec_ref[1, 5] = ones

    lane = lax.broadcasted_iota(jnp.int32, (RW_HD, lanes), 1)
    q = lanes // 4
    even = (lane // q) % 2 == 0
    low = lane < 2 * q
    low_v = lax.broadcasted_iota(jnp.int32, (RW_VH, lanes), 1) < 2 * q

    def pairs(x):
        xs = pltpu.roll(x, 2 * q, axis=1)
        return jnp.where(low, x, xs), jnp.where(low, xs, x)

    def merge(f, b):
        return (jnp.where(even, f, pltpu.roll(b, q, axis=1)),
                jnp.where(even, pltpu.roll(f, lanes - q, axis=1), b))

    def prepare(i, slot):
        fr = pl.ds(pl.multiple_of(i * RW_HD, RW_HD), RW_HD)
        br = pl.ds(pl.multiple_of((ts - 1 - i) * RW_HD, RW_HD), RW_HD)
        f01, f23 = pairs(fa_ref[fr, :])
        b01, b23 = pairs(ba_ref[br, :])
        w, bt = merge(f01, b01)
        kd, a = merge(f23, b23)
        r, vk = merge(fc_ref[fr, :], bc_ref[br, :])
        p_prev = vec_ref[1 - slot, 5]
        p = p_prev * w
        inv = 1.0 / p
        vec_ref[slot, 0] = bt * inv
        vec_ref[slot, 1] = kd * inv
        vec_ref[slot, 2] = a * p_prev
        vec_ref[slot, 3] = r * p
        vec_ref[slot, 4, :RW_VH] = jnp.where(low_v, vk[:RW_VH], vk[RW_VH:])
        vec_ref[slot, 5] = p

    def all_sum8(x):
        x = x + pltpu.roll(x, 4, axis=0)
        x = x + pltpu.roll(x, 2, axis=0)
        return x + pltpu.roll(x, 1, axis=0)

    kblocks = [slice(kb * 8, (kb + 1) * 8) for kb in range(RW_HD // 8)]

    def update(i, slot):
        for v0 in range(0, RW_VH, RW_VGROUP):
            vis = range(v0, v0 + RW_VGROUP)
            acc = {}
            for kb in kblocks:
                a_kb = vec_ref[slot, 2, kb, :]
                for vi in vis:
                    prod = s_ref[vi, kb, :] * a_kb
                    acc[vi] = prod if vi not in acc else acc[vi] + prod
            sa = {vi: all_sum8(acc[vi]) for vi in vis}
            yacc = {}
            for kb in kblocks:
                b_kb, k_kb, r_kb = vec_ref[slot, 0, kb, :], vec_ref[slot, 1, kb, :], vec_ref[slot, 3, kb, :]
                for vi in vis:
                    s = s_ref[vi, kb, :] + sa[vi] * b_kb + vec_ref[slot, 4, vi:vi + 1, :] * k_kb
                    s_ref[vi, kb, :] = s
                    prod = s * r_kb
                    yacc[vi] = prod if vi not in yacc else yacc[vi] + prod
            for vi in vis:
                y = all_sum8(yacc[vi])[0:1]
                yf_ref[pl.ds(i * RW_VH + vi, 1), :] = y
                yb_ref[pl.ds((ts - 1 - i) * RW_VH + vi, 1), :] = y

    prepare(0, 0)

    def two_steps(h, carry):
        i = 2 * h
        prepare(i + 1, 1)
        update(i, 0)
        prepare(jnp.minimum(i + 2, ts - 1), 0)
        update(i + 1, 1)
        return carry

    lax.fori_loop(0, ts // 2, two_steps, 0)


def _from_chains_kernel(yf_ref, yb_ref, of_ref, ob_ref, scr_ref, *, nb, tt):
    q = nb * RW_H
    for d, (y_ref, o_ref) in enumerate(((yf_ref, of_ref), (yb_ref, ob_ref))):
        for vg in range(RW_VH // 8):
            blk = jnp.swapaxes(y_ref[:, vg * 8:(vg + 1) * 8, :], 0, 1)
            blk = jnp.swapaxes(jnp.swapaxes(blk, 1, 2), 0, 1)
            for vh in range(2):
                src = (vh * 2 + d) * q
                scr_ref[:, vh * RW_VH + vg * 8:vh * RW_VH + (vg + 1) * 8, :] = blk[src:src + q]
        for b in range(nb):
            o_ref[b] = scr_ref[b * RW_H:(b + 1) * RW_H].reshape(RW_W, tt).T


def _rw_scan(r, v, a, w, b, kd, ts, tt, ctx_len):
    nb, nt, _ = r.shape
    lanes = RW_SLOTS * nb * RW_H
    one = lambda x: x.reshape(1, nb, nt, RW_W)
    r1, v1, a1 = one(r), one(v), one(a)
    fa = _to_chains([(w, 0), (b, 0), (kd, 0), (a1, 0)], tt)
    ba = _to_chains([(w, 1), (b, 1), (kd, 1), (a1, 0)], tt)
    fc = _to_chains([(r1, 0), (v1, 0)], tt)
    n_ctx = ctx_len // ts
    n_all = nt // ts
    fwd = lambda g: (g, 0)
    bwd = lambda g: (jnp.where(g < n_ctx, n_ctx - 1 - g, n_all - 1 - g + n_ctx), 0)
    kin = lambda m: pl.BlockSpec((ts * RW_HD, lanes), m)
    yout = lambda m: pl.BlockSpec((ts * RW_VH, lanes), m)
    yshape = jax.ShapeDtypeStruct((nt * RW_VH, lanes), F32)
    yf, yb = pl.pallas_call(
        functools.partial(_rw_scan_kernel, ts=ts, lanes=lanes),
        grid=(n_all,),
        in_specs=[kin(fwd), kin(fwd), kin(bwd), kin(bwd)],
        out_specs=[yout(fwd), yout(bwd)],
        out_shape=[yshape, yshape],
        scratch_shapes=[pltpu.VMEM((RW_VH, RW_HD, lanes), F32), pltpu.VMEM((2, 6, RW_HD, lanes), F32)],
        compiler_params=_params(("arbitrary",)),
    )(fa, fc, ba, fc)
    nat = jax.ShapeDtypeStruct((nb, nt, RW_W), F32)
    return pl.pallas_call(
        functools.partial(_from_chains_kernel, nb=nb, tt=tt),
        grid=(nt // tt,),
        in_specs=[pl.BlockSpec((tt, RW_VH, lanes), lambda i: (i, 0, 0))] * 2,
        out_specs=[pl.BlockSpec((nb, tt, RW_W), lambda i: (0, i, 0))] * 2,
        out_shape=[nat, nat],
        scratch_shapes=[pltpu.VMEM((nb * RW_H, RW_HD, tt), F32)],
        compiler_params=_params(("parallel",)),
    )(yf.reshape(nt, RW_VH, lanes), yb.reshape(nt, RW_VH, lanes))


def _outproj_kernel(x_ref, mod_ref, ys_ref, at_ref, wf_ref, wb_ref, bon_ref, g_ref,
                    gw_ref, gb_ref, sg_ref, lg_ref, lb_ref, avg_ref, wo_ref, o_ref, *, tm, ctx_len):
    j = pl.program_id(1)
    row = j * tm + lax.broadcasted_iota(jnp.int32, (tm, 1), 0)
    is_ctx = row < ctx_len
    a = jax.nn.gelu(ys_ref[0])
    o1 = _rms(a * _sigmoid(_dot_x3(a, gw_ref[...]) + gb_ref[...])) * sg_ref[...]
    wkv = wf_ref[0] + wb_ref[0]
    avg = avg_ref[...]
    cen = wkv - _seg_dot(wkv, avg)
    var = _seg_dot(cen * cen, avg)
    o3 = (cen * lax.rsqrt(var + RW_LN_EPS) * lg_ref[...] + lb_ref[...] + bon_ref[0]) * g_ref[0]
    o = (_dot(o1.astype(BF16), wo_ref[0:S5_W, :])
         + _dot(at_ref[0].astype(BF16), wo_ref[S5_W:S5_W + ATT_W, :])
         + _dot(o3.astype(BF16), wo_ref[S5_W + ATT_W:, :]))
    o_ref[0] = x_ref[0] + _pick_mod(mod_ref[0], is_ctx, 2) * o


def _outproj(x, mod, ys5, att, wkv_f, wkv_b, bonus, g, glu_w, glu_b, s5_out_g, ln_g, ln_b, w_out, tm, ctx_len):
    nb, nt, d = x.shape
    row = lambda a: a.reshape(1, -1)
    avg = _block_diag_mean(RW_W, RW_HD)
    params = (glu_w, row(glu_b), row(s5_out_g), row(ln_g), row(ln_b), avg, w_out.astype(BF16))
    full = lambda a: pl.BlockSpec(a.shape, lambda b, j: (0,) * a.ndim)
    tok = lambda w: pl.BlockSpec((1, tm, w), lambda b, j: (b, j, 0))
    return pl.pallas_call(
        functools.partial(_outproj_kernel, tm=tm, ctx_len=ctx_len),
        grid=(nb, nt // tm),
        in_specs=[tok(d), pl.BlockSpec((1, 2, N_MOD * d), lambda b, j: (b, 0, 0)),
                  tok(S5_W), tok(ATT_W), tok(RW_W), tok(RW_W), tok(RW_W), tok(RW_W)] + [full(a) for a in params],
        out_specs=tok(d),
        out_shape=jax.ShapeDtypeStruct(x.shape, F32),
        compiler_params=_params(("parallel", "parallel")),
    )(x, mod, ys5, att, wkv_f, wkv_b, bonus, g, *params)


FFN_HALO = 16
FFN_TF = 256


def _ffn_kernel(x_ref, xp_ref, xn_ref, mod_ref, g_ref, upg_ref, upv_ref, cwg_ref, cwv_ref, cbg_ref, cbv_ref,
                dn_ref, o_ref, hs_ref, acc_ref, *, tm, ctx_len, nt):
    j = pl.program_id(1)
    f = pl.program_id(2)
    hl, tf = FFN_HALO, FFN_TF
    mod = mod_ref[0]

    def adaln(xv, first_row):
        rows = first_row + lax.broadcasted_iota(jnp.int32, (xv.shape[0], 1), 0)
        is_ctx = rows < ctx_len
        h = _rms(xv) * g_ref[...]
        return (h * (1.0 + _pick_mod(mod, is_ctx, 4)) + _pick_mod(mod, is_ctx, 3)).astype(BF16)

    @pl.when(f == 0)
    def _():
        hs_ref[0:hl, :] = adaln(xp_ref[0], j * tm - hl)
        hs_ref[hl:hl + tm, :] = adaln(x_ref[0], j * tm)
        hs_ref[hl + tm:, :] = adaln(xn_ref[0], (j + 1) * tm)
        acc_ref[...] = jnp.zeros_like(acc_ref)

    n_ext = tm + 2 * hl
    row = j * tm + lax.broadcasted_iota(jnp.int32, (tm, 1), 0)
    no_prev = (row == 0) | (row == ctx_len)
    no_next = (row == ctx_len - 1) | (row == nt - 1)
    hs = hs_ref[...]

    def conv(up_ref, cw_ref, cb_ref):
        u = _dot(hs, up_ref[...])
        prev = jnp.where(no_prev, 0.0, pltpu.roll(u, 1, axis=0)[hl:hl + tm])
        nxt = jnp.where(no_next, 0.0, pltpu.roll(u, n_ext - 1, axis=0)[hl:hl + tm])
        cw = cw_ref[...]
        return cw[0:1] * prev + cw[1:2] * u[hl:hl + tm] + cw[2:3] * nxt + cb_ref[...]

    half = 0.5 * conv(upg_ref, cwg_ref, cbg_ref)
    act = (half + half * jnp.tanh(half)) * conv(upv_ref, cwv_ref, cbv_ref)
    acc_ref[...] += _dot(act.astype(BF16), dn_ref[...])

    @pl.when(f == pl.num_programs(2) - 1)
    def _():
        rows = j * tm + lax.broadcasted_iota(jnp.int32, (tm, 1), 0)
        o_ref[0] = x_ref[0] + _pick_mod(mod, rows < ctx_len, 5) * acc_ref[...]


def _conv_ffn(x, mod, g, up, conv_w, conv_b, down, tm, ctx_len):
    nb, nt, d = x.shape
    tf, hl = FFN_TF, FFN_HALO
    nf = D_FF // tf
    upb = up.astype(BF16)
    cb = conv_b.reshape(1, -1)
    nh = nt // hl
    cols = lambda rows: [pl.BlockSpec((rows, tf), lambda b, j, f: (0, f)),
                         pl.BlockSpec((rows, tf), lambda b, j, f: (0, nf + f))]
    return pl.pallas_call(
        functools.partial(_ffn_kernel, tm=tm, ctx_len=ctx_len, nt=nt),
        grid=(nb, nt // tm, nf),
        in_specs=[pl.BlockSpec((1, tm, d), lambda b, j, f: (b, j, 0)),
                  pl.BlockSpec((1, hl, d), lambda b, j, f: (b, jnp.maximum(j * (tm // hl) - 1, 0), 0)),
                  pl.BlockSpec((1, hl, d), lambda b, j, f: (b, jnp.minimum((j + 1) * (tm // hl), nh - 1), 0)),
                  pl.BlockSpec((1, 2, N_MOD * d), lambda b, j, f: (b, 0, 0)),
                  pl.BlockSpec((1, d), lambda b, j, f: (0, 0))] + cols(d) + cols(3) + cols(1) + [
                  pl.BlockSpec((tf, d), lambda b, j, f: (f, 0))],
        out_specs=pl.BlockSpec((1, tm, d), lambda b, j, f: (b, j, 0)),
        out_shape=jax.ShapeDtypeStruct(x.shape, F32),
        scratch_shapes=[pltpu.VMEM((tm + 2 * hl, d), BF16), pltpu.VMEM((tm, d), F32)],
        compiler_params=_params(("parallel", "parallel", "arbitrary")),
    )(x, x, x, mod, g.reshape(1, d), upb, upb, conv_w, conv_w, cb, cb, down.astype(BF16))


def _final_kernel(x_ref, g_ref, o_ref):
    o_ref[0] = _rms(x_ref[0]) * g_ref[...]


def _final_norm(x, g, ctx_len, tm):
    nb, nt, d = x.shape
    off = ctx_len // tm
    return pl.pallas_call(
        _final_kernel,
        grid=(nb, (nt - ctx_len) // tm),
        in_specs=[pl.BlockSpec((1, tm, d), lambda b, j: (b, j + off, 0)),
                  pl.BlockSpec((1, d), lambda b, j: (0, 0))],
        out_specs=pl.BlockSpec((1, tm, d), lambda b, j: (b, j, 0)),
        out_shape=jax.ShapeDtypeStruct((nb, nt - ctx_len, d), F32),
        compiler_params=_params(("parallel", "parallel")),
    )(x, g.reshape(1, d))


def _forward(p, *, tm, tmf, tq, ts, tt, tfin):
    x, ctx = p["x"], p["ctx"]
    ctx_len = ctx.shape[1]
    depth = p["mod_w"].shape[0]
    xa = jnp.concatenate([ctx, x], axis=1)
    mods = _modulations(p["c"], p["c_ctx"], p["mod_w"], p["mod_b"])
    for l in range(depth):
        mod = mods[l]
        zs5, zq, zkv, zrw = _inproj(xa, mod, p["norm1_g"][l].reshape(1, -1), p["w_in"][l], tm, ctx_len)
        ops = _s5_operators(p["s5_a_re"][l], p["s5_a_im"][l], p["s5_log_step"][l], p["s5_b_re"][l],
                            p["s5_b_im"][l], p["s5_c_re"][l], p["s5_c_im"][l], p["s5_d"][l])
        ys5 = _s5_scan(zs5, ops, ctx_len)
        att = _attention(zq, zkv, p["att_qn_g"][l], p["att_kn_g"][l], p["att_out_g"][l], tq, ctx_len)
        r, v, a, w, b, kd, g, bonus = _rw_prep(zrw, p["rw_mu"][l], p["rw_w0"][l], p["rw_w2"][l], p["rw_a0"][l],
                                                 p["rw_a2"][l], p["rw_g2"][l], p["rw_k_k"][l], p["rw_k_a"][l],
                                                 p["rw_r_k"][l], tm, ctx_len)
        wkv_f, wkv_b = _rw_scan(r, v, a, w, b, kd, ts, tt, ctx_len)
        xa = _outproj(xa, mod, ys5, att, wkv_f, wkv_b, bonus, g, p["s5_glu_w"][l], p["s5_glu_b"][l],
                      p["s5_out_g"][l], p["rw_ln_g"][l], p["rw_ln_b"][l], p["w_out"][l], tm, ctx_len)
        xa = _conv_ffn(xa, mod, p["norm2_g"][l], p["ffn_up"][l], p["ffn_conv_w"][l], p["ffn_conv_b"][l],
                       p["ffn_down"][l], tmf, ctx_len)
    return _final_norm(xa, p["final_g"], ctx_len, tfin)


_ARG_NAMES = ("x c ctx c_ctx norm1_g norm2_g mod_w mod_b w_in w_out s5_a_re s5_a_im s5_log_step s5_b_re s5_b_im "
              "s5_c_re s5_c_im s5_d s5_glu_w s5_glu_b s5_out_g att_qn_g att_kn_g att_out_g rw_mu rw_w0 rw_w2 "
              "rw_a0 rw_a2 rw_g2 rw_k_k rw_k_a rw_r_k rw_ln_g rw_ln_b ffn_up ffn_conv_w ffn_conv_b ffn_down "
              "final_g").split()


def kernel(x, c, ctx, c_ctx, norm1_g, norm2_g, mod_w, mod_b, w_in, w_out, s5_a_re, s5_a_im, s5_log_step, s5_b_re, s5_b_im, s5_c_re, s5_c_im, s5_d, s5_glu_w, s5_glu_b, s5_out_g, att_qn_g, att_kn_g, att_out_g, rw_mu, rw_w0, rw_w2, rw_a0, rw_a2, rw_g2, rw_k_k, rw_k_a, rw_r_k, rw_ln_g, rw_ln_b, ffn_up, ffn_conv_w, ffn_conv_b, ffn_down, final_g):
    args = (x, c, ctx, c_ctx, norm1_g, norm2_g, mod_w, mod_b, w_in, w_out, s5_a_re, s5_a_im, s5_log_step, s5_b_re,
            s5_b_im, s5_c_re, s5_c_im, s5_d, s5_glu_w, s5_glu_b, s5_out_g, att_qn_g, att_kn_g, att_out_g, rw_mu,
            rw_w0, rw_w2, rw_a0, rw_a2, rw_g2, rw_k_k, rw_k_a, rw_r_k, rw_ln_g, rw_ln_b, ffn_up, ffn_conv_w,
            ffn_conv_b, ffn_down, final_g)
    return _forward(dict(zip(_ARG_NAMES, args)), tm=768, tmf=1152, tq=256, ts=32, tt=128, tfin=256)
```

```python
import functools
import math

import jax
import jax.numpy as jnp
from jax import lax
from jax.experimental import pallas as pl
from jax.experimental.pallas import tpu as pltpu

F32 = jnp.float32
BF16 = jnp.bfloat16
HI = lax.Precision.HIGHEST

D_MODEL = 1024
GRID_W = 64
S5_W = 256
S5_GH = 16
S5_G = 16
S5_P = 64
HEAD_DIM = 64
ATT_W = 512
N_Q = 8
GQA_REP = 4
N_KV = 2
ATT_KV = 128
ROPE_PAIRS = 16
ROPE_THETA = 10000.0
RW_W = 256
RW_HD = 64
RW_H = 4
LORA_W = 32
LORA_A = 32
LORA_G = 64
RW_IN = 960
RW_PAD = 1024
RW_LN_EPS = 64e-5
D_FF = 2816
N_MOD = 6
RMS_EPS = 1e-6

S5_T = 32
S5_CB = 8
V7X_VMEM_LIMIT = 56 * 1024 * 1024


def _params(sem, vmem=V7X_VMEM_LIMIT):
    return pltpu.CompilerParams(dimension_semantics=sem, vmem_limit_bytes=vmem)


def _rms(x):
    return x * lax.rsqrt(jnp.mean(x * x, axis=-1, keepdims=True) + RMS_EPS)


def _sigmoid(x):
    return 1.0 / (1.0 + jnp.exp(-x))


def _dot(a, b, precision=None):
    return jnp.dot(a, b, precision=precision, preferred_element_type=F32)


def _split2(x):
    hi = x.astype(BF16)
    return hi, (x - hi.astype(F32)).astype(BF16)


def _dot_x3(x, w):
    xh, xl = _split2(x)
    wh, wl = _split2(w)
    return _dot(jnp.concatenate([xh, xl, xh], axis=1), jnp.concatenate([wh, wh, wl], axis=0))


def _seg_dot(x, m):
    xh, xl = _split2(x)
    mb = m.astype(BF16)
    return _dot(jnp.concatenate([xh, xl], axis=1), jnp.concatenate([mb, mb], axis=0))


def _block_diag_mean(width, seg):
    i = jnp.arange(width)
    return ((i[:, None] // seg) == (i[None, :] // seg)).astype(F32) / seg


def _mod_kernel(c_ref, w_ref, b_ref, o_ref):
    c = c_ref[...]
    s = c * _sigmoid(c)
    o_ref[0] = _dot(s, w_ref[0], HI) + b_ref[0]


def _modulations(c, c_ctx, mod_w, mod_b):
    depth, d, n = mod_w.shape
    nb = c.shape[0]
    rows = ((nb + 1 + 7) // 8) * 8
    cc = jnp.zeros((rows, d), F32).at[:nb].set(c).at[nb].set(c_ctx)
    tn = n // 4
    out = pl.pallas_call(
        _mod_kernel,
        grid=(depth, n // tn),
        in_specs=[pl.BlockSpec((rows, d), lambda l, j: (0, 0)),
                  pl.BlockSpec((1, d, tn), lambda l, j: (l, 0, j)),
                  pl.BlockSpec((1, 1, tn), lambda l, j: (l, 0, j))],
        out_specs=pl.BlockSpec((1, rows, tn), lambda l, j: (l, 0, j)),
        out_shape=jax.ShapeDtypeStruct((depth, rows, n), F32),
        compiler_params=_params(("parallel", "parallel")),
    )(cc, mod_w, mod_b.reshape(depth, 1, n))
    lat = out[:, :nb]
    ctx = jnp.broadcast_to(out[:, nb:nb + 1], lat.shape)
    return jnp.stack([lat, ctx], axis=2)


def _pick_mod(mod, is_ctx, idx):
    lo = idx * D_MODEL
    return jnp.where(is_ctx, mod[1:2, lo:lo + D_MODEL], mod[0:1, lo:lo + D_MODEL])


def _inproj_kernel(x_ref, mod_ref, g_ref, ws_ref, wq_ref, wkv_ref, wr_ref,
                   s5_ref, q_ref, kv_ref, rw_ref, *, tm, ctx_len):
    j = pl.program_id(1)
    x = x_ref[0]
    row = j * tm + lax.broadcasted_iota(jnp.int32, (tm, 1), 0)
    is_ctx = row < ctx_len
    mod = mod_ref[0]
    h = _rms(x) * g_ref[...]
    h = h * (1.0 + _pick_mod(mod, is_ctx, 1)) + _pick_mod(mod, is_ctx, 0)
    hb = h.astype(BF16)
    s5_ref[0] = _dot(hb, ws_ref[...])
    q_ref[0] = _dot(hb, wq_ref[...])
    kv_ref[0] = _dot(hb, wkv_ref[...])
    rw_ref[0] = _dot(hb, wr_ref[...])


def _inproj(x, mod, g, w_in, tm, ctx_len):
    nb, nt, d = x.shape
    wb = w_in.astype(BF16)
    ws = wb[:, :S5_W]
    wq = wb[:, S5_W:S5_W + ATT_W]
    wkv = wb[:, S5_W + ATT_W:S5_W + ATT_W + 2 * ATT_KV]
    wr = jnp.pad(wb[:, S5_W + ATT_W + 2 * ATT_KV:], ((0, 0), (0, RW_PAD - RW_IN)))
    full = lambda a: pl.BlockSpec(a.shape, lambda b, j: (0,) * a.ndim)
    tok = lambda w: pl.BlockSpec((1, tm, w), lambda b, j: (b, j, 0))
    return pl.pallas_call(
        functools.partial(_inproj_kernel, tm=tm, ctx_len=ctx_len),
        grid=(nb, nt // tm),
        in_specs=[tok(d), pl.BlockSpec((1, 2, N_MOD * d), lambda b, j: (b, 0, 0)),
                  full(g), full(ws), full(wq), full(wkv), full(wr)],
        out_specs=[tok(S5_W), tok(ATT_W), tok(2 * ATT_KV), tok(RW_PAD)],
        out_shape=[jax.ShapeDtypeStruct((nb, nt, w), F32) for w in (S5_W, ATT_W, 2 * ATT_KV, RW_PAD)],
        compiler_params=_params(("parallel", "parallel")),
    )(x, mod, g, ws, wq, wkv, wr)


def _s5_operators(a_re, a_im, log_step, b_re, b_im, c_re, c_im, d):
    T = S5_T
    lam = lax.complex(a_re.astype(F32), a_im.astype(F32))
    step = jnp.exp(log_step.astype(F32))[..., None]
    lam_bar = jnp.exp(lam * step)
    bmat = lax.complex(b_re.astype(F32), b_im.astype(F32))
    b_bar = ((lam_bar - 1.0) / lam)[..., None] * bmat
    cmat = lax.complex(c_re.astype(F32), c_im.astype(F32))
    n = jnp.arange(T + 1, dtype=F32)
    pw = jnp.exp((lam * step)[..., None] * n)
    cp = jnp.swapaxes(cmat, -1, -2)[:, :, :, None, :] * pw[..., None]
    bp = jnp.transpose(pw, (0, 1, 3, 2))[:, :, :, None, :] * jnp.transpose(b_bar, (0, 1, 3, 2))[:, :, None]
    G, P, GH = S5_G, S5_P, S5_GH
    W = T * GH

    def ri_rows(z):
        return jnp.concatenate([jnp.real(z), -jnp.imag(z)], axis=1)

    def ri_cols(z):
        return jnp.concatenate([jnp.real(z), jnp.imag(z)], axis=2)

    bt = jnp.stack([jnp.concatenate([jnp.real(jnp.swapaxes(b_bar[di], -1, -2)),
                                     jnp.imag(jnp.swapaxes(b_bar[di], -1, -2))], axis=-1)
                    for di in range(2)], axis=1)
    cpk0 = ri_rows(cp[0, :, :, :T].reshape(G, P, W))
    cpk1 = ri_rows(cp[1, :, :, :T][:, :, ::-1].reshape(G, P, W))
    cc = jnp.concatenate([ri_rows(cp[0, :, :, 1:].reshape(G, P, W)),
                          ri_rows(cp[1, :, :, 1:][:, :, ::-1].reshape(G, P, W))], axis=1)
    bc = jnp.concatenate([ri_cols(bp[0, :, :T][:, ::-1].reshape(G, W, P)),
                          ri_cols(bp[1, :, :T].reshape(G, W, P))], axis=2)
    lt = pw[..., T]
    lam_rows = jnp.stack([jnp.concatenate([jnp.real(lt[0]), jnp.real(lt[0])], -1),
                          jnp.concatenate([-jnp.imag(lt[0]), jnp.imag(lt[0])], -1),
                          jnp.concatenate([jnp.real(lt[1]), jnp.real(lt[1])], -1),
                          jnp.concatenate([-jnp.imag(lt[1]), jnp.imag(lt[1])], -1)], axis=1)
    dvec = jnp.tile(d.astype(F32).reshape(G, 1, GH), (1, 1, T))
    return bt, cpk0, cpk1, bc, cc, lam_rows, dvec


def _s5_kernel(u_ref, bt_ref, cp0_ref, cp1_ref, bc_ref, cc_ref, lam_ref, d_ref, y_ref,
               kt_ref, spf_ref, spb_ref, ef_ref, eb_ref, *, n_ctx, n_chunks, nb):
    T, GH, P2 = S5_T, S5_GH, 2 * S5_P
    W = T * GH
    krow0 = _dot(bt_ref[0, 0], cp0_ref[0], HI)
    krow1 = _dot(bt_ref[0, 1], cp1_ref[0], HI)
    lane = lax.broadcasted_iota(jnp.int32, (GH, W), 1)
    for s in range(T):
        f = krow0 if s == 0 else pltpu.roll(krow0, s * GH, axis=1)
        f = jnp.where(lane >= s * GH, f, 0.0)
        sh = (T - 1 - s) * GH
        b = krow1 if sh == 0 else pltpu.roll(krow1, W - sh, axis=1)
        b = jnp.where(lane < (s + 1) * GH, b, 0.0)
        kt_ref[s * GH:(s + 1) * GH, :] = f + b
    u = u_ref[0]
    y = u * d_ref[0] + _dot_x3(u, kt_ref[...])
    e = _dot_x3(u, bc_ref[0])
    ef_ref[...] = e[:, :P2]
    eb_ref[...] = e[:, P2:]
    lam = lam_ref[0]

    def run(order, e_ref, sp_ref, ra, rb):
        s = jnp.zeros((nb, P2), F32)
        for c in order:
            rows = pl.ds((c // S5_CB) * S5_CB * nb + c % S5_CB, nb, stride=S5_CB)
            sp_ref[rows, :] = s
            s = ra * s + rb * pltpu.roll(s, S5_P, axis=1) + e_ref[rows, :]

    run(list(range(n_chunks)), ef_ref, spf_ref, lam[0:1], lam[1:2])
    run(list(range(n_ctx - 1, -1, -1)) + list(range(n_chunks - 1, n_ctx - 1, -1)), eb_ref, spb_ref,
        lam[2:3], lam[3:4])
    sp = jnp.concatenate([spf_ref[...], spb_ref[...]], axis=1)
    y_ref[0] = y + _dot_x3(sp, cc_ref[0])


S5_GL = 128 // S5_GH


def _s5_in_kernel(z_ref, u_ref, *, nb):
    T, GH, CB = S5_T, S5_GH, S5_CB

    def one_batch(b, carry):
        rows = pl.ds(pl.multiple_of(b * CB, CB), CB)
        for t in range(T):
            zt = z_ref[b, pl.ds(t, CB, stride=T), :]
            for g in range(S5_GL):
                u_ref[g, rows, t * GH:(t + 1) * GH] = zt[:, g * GH:(g + 1) * GH]
        return carry

    lax.fori_loop(0, nb, one_batch, 0)


def _s5_out_kernel(y_ref, o_ref, row_ref, *, nb):
    T, GH, CB = S5_T, S5_GH, S5_CB

    def one_batch(b, carry):
        rows = pl.ds(pl.multiple_of(b * CB, CB), CB)
        for t in range(T):
            for g in range(S5_GL):
                row_ref[:, g * GH:(g + 1) * GH] = y_ref[g, rows, t * GH:(t + 1) * GH]
            o_ref[b, pl.ds(t, CB, stride=T), :] = row_ref[...]
        return carry

    lax.fori_loop(0, nb, one_batch, 0)


def _s5_scan(zs5, ops, ctx_len):
    nb, nt, _ = zs5.shape
    T, G, GH, CB, GL = S5_T, S5_G, S5_GH, S5_CB, S5_GL
    W = T * GH
    n_chunks = nt // T
    n_blocks = n_chunks // CB
    rows = n_chunks * nb
    relayout_grid = (n_blocks, G // GL)
    nat = pl.BlockSpec((nb, CB * T, GL * GH), lambda i, h: (0, i, h))
    chunked = pl.BlockSpec((GL, nb * CB, W), lambda i, h: (h, i, 0))
    u = pl.pallas_call(
        functools.partial(_s5_in_kernel, nb=nb),
        grid=relayout_grid,
        in_specs=[nat],
        out_specs=chunked,
        out_shape=jax.ShapeDtypeStruct((G, rows, W), F32),
        compiler_params=_params(("parallel", "parallel")),
    )(zs5)
    bt, cpk0, cpk1, bc, cc, lam_rows, dvec = ops
    per_g = lambda a: pl.BlockSpec((1,) + a.shape[1:], lambda g: (g,) + (0,) * (a.ndim - 1))
    y = pl.pallas_call(
        functools.partial(_s5_kernel, n_ctx=ctx_len // T, n_chunks=n_chunks, nb=nb),
        grid=(G,),
        in_specs=[per_g(u), per_g(bt), per_g(cpk0), per_g(cpk1), per_g(bc), per_g(cc),
                  per_g(lam_rows), per_g(dvec)],
        out_specs=per_g(u),
        out_shape=jax.ShapeDtypeStruct(u.shape, F32),
        scratch_shapes=[pltpu.VMEM((W, W), F32)] + [pltpu.VMEM((rows, 2 * S5_P), F32)] * 4,
        compiler_params=_params(("parallel",)),
    )(u, bt, cpk0, cpk1, bc, cc, lam_rows, dvec)
    return pl.pallas_call(
        functools.partial(_s5_out_kernel, nb=nb),
        grid=relayout_grid,
        in_specs=[chunked],
        out_specs=nat,
        out_shape=jax.ShapeDtypeStruct(zs5.shape, F32),
        scratch_shapes=[pltpu.VMEM((CB, GL * GH), F32)],
        compiler_params=_params(("parallel", "parallel")),
    )(y)


def _rope_tables(nt, ctx_len, width):
    tok = jnp.arange(nt - ctx_len, dtype=jnp.int32)
    pos_row = (tok // GRID_W).astype(F32)
    pos_col = (tok % GRID_W).astype(F32)
    inv = ROPE_THETA ** (-jnp.arange(ROPE_PAIRS, dtype=F32) / ROPE_PAIRS)
    ang_r = pos_row[:, None] * inv
    ang_c = pos_col[:, None] * inv
    ang = jnp.concatenate([ang_r, ang_r, ang_c, ang_c], axis=1)
    sign = jnp.tile(jnp.concatenate([-jnp.ones(ROPE_PAIRS, F32), jnp.ones(ROPE_PAIRS, F32)]), 2)
    cos = jnp.concatenate([jnp.ones((ctx_len, HEAD_DIM), F32), jnp.cos(ang)], axis=0)
    sin = jnp.concatenate([jnp.zeros((ctx_len, HEAD_DIM), F32), jnp.sin(ang) * sign], axis=0)
    reps = width // HEAD_DIM
    return jnp.tile(cos, (1, reps)), jnp.tile(sin, (1, reps))


def _head_norm_rope(t, pmat, g, cos, sin):
    width = t.shape[-1]
    ms = _seg_dot(t * t, pmat)
    tn = t * lax.rsqrt(ms + RMS_EPS) * g
    lane = lax.broadcasted_iota(jnp.int32, tn.shape, 1)
    first = (lane % (2 * ROPE_PAIRS)) < ROPE_PAIRS
    partner = jnp.where(first, pltpu.roll(tn, width - ROPE_PAIRS, axis=1), pltpu.roll(tn, ROPE_PAIRS, axis=1))
    return tn * cos + partner * sin


def _attn_kernel(q_ref, kv_ref, cq_ref, sq_ref, ck_ref, sk_ref, qg_ref, kg_ref, og_ref, pq_ref, pk_ref,
                 o_ref, kn_ref, vb_ref, *, tq, ctx_len, nk):
    j = pl.program_id(1)

    @pl.when(j == 0)
    def _():
        kv = kv_ref[0]
        kr = _head_norm_rope(kv[:, :ATT_KV], pk_ref[...], kg_ref[...], ck_ref[...], sk_ref[...])
        ones = jnp.ones((kv.shape[0], HEAD_DIM), BF16)
        for g in range(N_KV):
            kn_ref[g] = kr[:, g * HEAD_DIM:(g + 1) * HEAD_DIM].astype(BF16)
            vb_ref[g, :, :HEAD_DIM] = kv[:, ATT_KV + g * HEAD_DIM:ATT_KV + (g + 1) * HEAD_DIM].astype(BF16)
            vb_ref[g, :, HEAD_DIM:] = ones

    qr = _head_norm_rope(q_ref[0], pq_ref[...], qg_ref[...], cq_ref[...], sq_ref[...]) * (
        HEAD_DIM ** -0.5 * math.log2(math.e))

    def attend(n_keys):
        outs = []
        for h in range(N_Q):
            g = h // GQA_REP
            qh = qr[:, h * HEAD_DIM:(h + 1) * HEAD_DIM].astype(BF16)
            s = lax.dot_general(qh, kn_ref[g, :n_keys, :], (((1,), (1,)), ((), ())),
                                preferred_element_type=F32)
            p = jnp.exp2(s - jnp.max(s, axis=-1, keepdims=True))
            ov = _dot(p.astype(BF16), vb_ref[g, :n_keys, :])
            outs.append(ov[:, :HEAD_DIM] / ov[:, HEAD_DIM:])
        o = jnp.concatenate(outs, axis=1)
        o_ref[0] = _rms(o) * og_ref[...]

    @pl.when(j * tq < ctx_len)
    def _():
        attend(ctx_len)

    @pl.when(j * tq >= ctx_len)
    def _():
        attend(nk)


def _attention(zq, zkv, qn_g, kn_g, out_g, tq, ctx_len):
    nb, nt, _ = zq.shape
    cq, sq = _rope_tables(nt, ctx_len, ATT_W)
    ck, sk = cq[:, :ATT_KV], sq[:, :ATT_KV]
    qg = jnp.tile(qn_g, N_Q).reshape(1, ATT_W)
    kg = jnp.tile(kn_g, N_KV).reshape(1, ATT_KV)
    pq = _block_diag_mean(ATT_W, HEAD_DIM)
    pk = _block_diag_mean(ATT_KV, HEAD_DIM)
    full = lambda a: pl.BlockSpec(a.shape, lambda b, j: (0,) * a.ndim)
    return pl.pallas_call(
        functools.partial(_attn_kernel, tq=tq, ctx_len=ctx_len, nk=nt),
        grid=(nb, nt // tq),
        in_specs=[pl.BlockSpec((1, tq, ATT_W), lambda b, j: (b, j, 0)),
                  pl.BlockSpec((1, nt, 2 * ATT_KV), lambda b, j: (b, 0, 0)),
                  pl.BlockSpec((tq, ATT_W), lambda b, j: (j, 0)),
                  pl.BlockSpec((tq, ATT_W), lambda b, j: (j, 0)),
                  full(ck), full(sk), full(qg), full(kg),
                  pl.BlockSpec((1, ATT_W), lambda b, j: (0, 0)), full(pq), full(pk)],
        out_specs=pl.BlockSpec((1, tq, ATT_W), lambda b, j: (b, j, 0)),
        out_shape=jax.ShapeDtypeStruct((nb, nt, ATT_W), F32),
        scratch_shapes=[pltpu.VMEM((N_KV, nt, HEAD_DIM), BF16), pltpu.VMEM((N_KV, nt, 2 * HEAD_DIM), BF16)],
        compiler_params=_params(("parallel", "arbitrary")),
    )(zq, zkv, cq, sq, ck, sk, qg, kg, out_g.reshape(1, ATT_W), pq, pk)


def _rw_prep_kernel(z_ref, zp_ref, zn_ref, mu_ref, wl_ref, w0_ref, a0_ref, kk_ref, ka_ref, rk_ref, ones_ref,
                    r_ref, v_ref, a_ref, w_ref, b_ref, kd_ref, g_ref, bon_ref, *, tm, ctx_len, nt):
    j = pl.program_id(1)
    z = z_ref[0]
    row_l = lax.broadcasted_iota(jnp.int32, (tm, 1), 0)
    row = j * tm + row_l
    prev = jnp.where(row_l == 0, zp_ref[0, 7:8, :], pltpu.roll(z, 1, axis=0))
    prev = jnp.where((row == 0) | (row == ctx_len), 0.0, prev)
    nxt = jnp.where(row_l == tm - 1, zn_ref[0, 0:1, :], pltpu.roll(z, tm - 1, axis=0))
    nxt = jnp.where((row == ctx_len - 1) | (row == nt - 1), 0.0, nxt)
    zs = z + mu_ref[...] * (0.5 * (prev + nxt) - z)
    r = zs[:, 0:RW_W]
    k = zs[:, RW_W:2 * RW_W]
    v = zs[:, 2 * RW_W:3 * RW_W]
    lo = zs[:, 3 * RW_W:]
    lane = lax.broadcasted_iota(jnp.int32, lo.shape, 1)
    lo = jnp.where(lane < 2 * LORA_W, jnp.tanh(lo),
                   jnp.where(lane < 2 * LORA_W + 2 * LORA_A, lo, _sigmoid(lo)))
    proj = _dot_x3(lo, wl_ref[...])
    ones = ones_ref[...]
    kk = k * kk_ref[...]
    nrm = jnp.sqrt(_seg_dot(kk * kk, ones))
    kk = kk / jnp.maximum(nrm, 1e-12)
    r_ref[0] = r.T
    v_ref[0] = v.T
    a_ref[0] = (-kk).T
    g_ref[0] = proj[:, 4 * RW_W:5 * RW_W]
    bon = jnp.zeros_like(r)
    for di in range(2):
        pre = w0_ref[di:di + 1, :] + proj[:, di * RW_W:(di + 1) * RW_W]
        w_log = -(jnp.maximum(-pre, 0.0) + jnp.log(1.0 + jnp.exp(-jnp.abs(pre)))) - 0.5
        w_ref[di, 0] = jnp.exp(-jnp.exp(w_log)).T
        iclr = _sigmoid(a0_ref[di:di + 1, :] + proj[:, (2 + di) * RW_W:(3 + di) * RW_W])
        kd = k * (1.0 + (iclr - 1.0) * ka_ref[...])
        kd_ref[di, 0] = kd.T
        b_ref[di, 0] = (kk * iclr).T
        bon = bon + _seg_dot(r * kd * rk_ref[...], ones) * v
    bon_ref[0] = bon


def _rw_prep(zrw, mu, w0, w2, a0, a2, g2, k_k, k_a, r_k, tm, ctx_len):
    nb, nt, _ = zrw.shape
    wl = jnp.zeros((RW_PAD - 3 * RW_W, 5 * RW_W), F32)
    wl = wl.at[0:LORA_W, 0:RW_W].set(w2[0]).at[LORA_W:2 * LORA_W, RW_W:2 * RW_W].set(w2[1])
    o = 2 * LORA_W
    wl = wl.at[o:o + LORA_A, 2 * RW_W:3 * RW_W].set(a2[0]).at[o + LORA_A:o + 2 * LORA_A, 3 * RW_W:4 * RW_W].set(a2[1])
    o = 2 * LORA_W + 2 * LORA_A
    wl = wl.at[o:o + LORA_G, 4 * RW_W:5 * RW_W].set(g2)
    mu_p = jnp.pad(mu, (0, RW_PAD - RW_IN)).reshape(1, RW_PAD)
    ones = _block_diag_mean(RW_W, RW_HD) * RW_HD
    row = lambda a: a.reshape(1, RW_W)
    full = lambda a: pl.BlockSpec(a.shape, lambda b, j: (0,) * a.ndim)
    tok = pl.BlockSpec((1, tm, RW_W), lambda b, j: (b, j, 0))
    one = jax.ShapeDtypeStruct((nb, nt, RW_W), F32)
    cm = pl.BlockSpec((1, RW_W, tm), lambda b, j: (b, 0, j))
    cm2 = pl.BlockSpec((2, 1, RW_W, tm), lambda b, j: (0, b, 0, j))
    cm_one = jax.ShapeDtypeStruct((nb, RW_W, nt), F32)
    cm_two = jax.ShapeDtypeStruct((2, nb, RW_W, nt), F32)
    n8 = nt // 8
    args = (zrw, zrw, zrw, mu_p, wl, w0, a0, row(k_k), row(k_a), row(r_k), ones)
    return pl.pallas_call(
        functools.partial(_rw_prep_kernel, tm=tm, ctx_len=ctx_len, nt=nt),
        grid=(nb, nt // tm),
        in_specs=[pl.BlockSpec((1, tm, RW_PAD), lambda b, j: (b, j, 0)),
                  pl.BlockSpec((1, 8, RW_PAD), lambda b, j: (b, jnp.maximum(j * (tm // 8) - 1, 0), 0)),
                  pl.BlockSpec((1, 8, RW_PAD), lambda b, j: (b, jnp.minimum((j + 1) * (tm // 8), n8 - 1), 0)),
                  ] + [full(a) for a in args[3:]],
        out_specs=[cm, cm, cm, cm2, cm2, cm2, tok, tok],
        out_shape=[cm_one, cm_one, cm_one, cm_two, cm_two, cm_two, one, one],
        compiler_params=_params(("parallel", "parallel")),
    )(*args)


RW_VH = RW_HD // 2
RW_SLOTS = 4
RW_VGROUP = 16


def _to_chains_kernel(*refs, nb, tt):
    *x_refs, o_ref, scr_ref = refs
    for j, ref in enumerate(x_refs):
        for b in range(nb):
            xt = ref[0, b]
            for h in range(RW_H):
                scr_ref[(j * nb + b) * RW_H + h] = xt[h * RW_HD:(h + 1) * RW_HD, :]
    reps = RW_SLOTS // len(x_refs)
    for kg in range(RW_HD // 8):
        blk = jnp.swapaxes(scr_ref[:, kg * 8:(kg + 1) * 8, :], 0, 1)
        blk = jnp.concatenate([blk] * reps, axis=1)
        o_ref[:, kg * 8:(kg + 1) * 8, :] = jnp.swapaxes(jnp.swapaxes(blk, 1, 2), 0, 1)


def _to_chains(xs, tt):
    _, nb, _, nt = xs[0][0].shape
    lanes = RW_SLOTS * nb * RW_H
    spec = lambda di: pl.BlockSpec((1, nb, RW_W, tt), lambda i: (di, 0, 0, i))
    out = pl.pallas_call(
        functools.partial(_to_chains_kernel, nb=nb, tt=tt),
        grid=(nt // tt,),
        in_specs=[spec(di) for _, di in xs],
        out_specs=pl.BlockSpec((tt, RW_HD, lanes), lambda i: (i, 0, 0)),
        out_shape=jax.ShapeDtypeStruct((nt, RW_HD, lanes), F32),
        scratch_shapes=[pltpu.VMEM((len(xs) * nb * RW_H, RW_HD, tt), F32)],
        compiler_params=_params(("parallel",)),
    )(*[x for x, _ in xs])
    return out.reshape(nt * RW_HD, lanes)


def _rw_scan_kernel(fa_ref, fc_ref, ba_ref, bc_ref, yf_ref, yb_ref, s_ref, vec_ref, *, ts, lanes):
    ones = jnp.ones((RW_HD, lanes), F32)

    @pl.when(pl.program_id(0) == 0)
    def _():
        s_ref[...] = jnp.zeros_like(s_ref)
        vec_ref[1, 5] = ones

    p_end = vec_ref[1, 5]
    for vi in range(RW_VH):
        s_ref[vi] = s_ref[vi] * p_end
    vec_ref[1, 5] = ones

    lane = lax.broadcasted_iota(jnp.int32, (RW_HD, lanes), 1)
    q = lanes // 4
    even = (lane // q) % 2 == 0
    low = lane < 2 * q
    low_v = lax.broadcasted_iota(jnp.int32, (RW_VH, lanes), 1) < 2 * q

    def pairs(x):
        xs = pltpu.roll(x, 2 * q, axis=1)
        return jnp.where(low, x, xs), jnp.where(low, xs, x)

    def merge(f, b):
        return (jnp.where(even, f, pltpu.roll(b, q, axis=1)),
                jnp.where(even, pltpu.roll(f, lanes - q, axis=1), b))

    def prepare(i, slot):
        fr = pl.ds(pl.multiple_of(i * RW_HD, RW_HD), RW_HD)
        br = pl.ds(pl.multiple_of((ts - 1 - i) * RW_HD, RW_HD), RW_HD)
        f01, f23 = pairs(fa_ref[fr, :])
        b01, b23 = pairs(ba_ref[br, :])
        w, bt = merge(f01, b01)
        kd, a = merge(f23, b23)
        r, vk = merge(fc_ref[fr, :], bc_ref[br, :])
        p_prev = vec_ref[1 - slot, 5]
        p = p_prev * w
        inv = 1.0 / p
        vec_ref[slot, 0] = bt * inv
        vec_ref[slot, 1] = kd * inv
        vec_ref[slot, 2] = a * p_prev
        vec_ref[slot, 3] = r * p
        vec_ref[slot, 4, :RW_VH] = jnp.where(low_v, vk[:RW_VH], vk[RW_VH:])
        vec_ref[slot, 5] = p

    def all_sum8(x):
        x = x + pltpu.roll(x, 4, axis=0)
        x = x + pltpu.roll(x, 2, axis=0)
        return x + pltpu.roll(x, 1, axis=0)

    kblocks = [slice(kb * 8, (kb + 1) * 8) for kb in range(RW_HD // 8)]

    def update(i, slot):
        for v0 in range(0, RW_VH, RW_VGROUP):
            vis = range(v0, v0 + RW_VGROUP)
            acc = {}
            for kb in kblocks:
                a_kb = vec_ref[slot, 2, kb, :]
                for vi in vis:
                    prod = s_ref[vi, kb, :] * a_kb
                    acc[vi] = prod if vi not in acc else acc[vi] + prod
            sa = {vi: all_sum8(acc[vi]) for vi in vis}
            yacc = {}
            for kb in kblocks:
                b_kb, k_kb, r_kb = vec_ref[slot, 0, kb, :], vec_ref[slot, 1, kb, :], vec_ref[slot, 3, kb, :]
                for vi in vis:
                    s = s_ref[vi, kb, :] + sa[vi] * b_kb + vec_ref[slot, 4, vi:vi + 1, :] * k_kb
                    s_ref[vi, kb, :] = s
                    prod = s * r_kb
                    yacc[vi] = prod if vi not in yacc else yacc[vi] + prod
            for vi in vis:
                y = all_sum8(yacc[vi])[0:1]
                yf_ref[pl.ds(i * RW_VH + vi, 1), :] = y
                yb_ref[pl.ds((ts - 1 - i) * RW_VH + vi, 1), :] = y

    prepare(0, 0)

    def two_steps(h, carry):
        i = 2 * h
        prepare(i + 1, 1)
        update(i, 0)
        prepare(jnp.minimum(i + 2, ts - 1), 0)
        update(i + 1, 1)
        return carry

    lax.fori_loop(0, ts // 2, two_steps, 0)


def _from_chains_kernel(yf_ref, yb_ref, of_ref, ob_ref, scr_ref, *, nb, tt):
    q = nb * RW_H
    for d, (y_ref, o_ref) in enumerate(((yf_ref, of_ref), (yb_ref, ob_ref))):
        for vg in range(RW_VH // 8):
            blk = jnp.swapaxes(y_ref[:, vg * 8:(vg + 1) * 8, :], 0, 1)
            blk = jnp.swapaxes(jnp.swapaxes(blk, 1, 2), 0, 1)
            for vh in range(2):
                src = (vh * 2 + d) * q
                scr_ref[:, vh * RW_VH + vg * 8:vh * RW_VH + (vg + 1) * 8, :] = blk[src:src + q]
        for b in range(nb):
            o_ref[b] = scr_ref[b * RW_H:(b + 1) * RW_H].reshape(RW_W, tt)


def _rw_scan(r, v, a, w, b, kd, ts, tt, ctx_len):
    nb, _, nt = r.shape
    lanes = RW_SLOTS * nb * RW_H
    one = lambda x: x.reshape(1, nb, RW_W, nt)
    r1, v1, a1 = one(r), one(v), one(a)
    fa = _to_chains([(w, 0), (b, 0), (kd, 0), (a1, 0)], tt)
    ba = _to_chains([(w, 1), (b, 1), (kd, 1), (a1, 0)], tt)
    fc = _to_chains([(r1, 0), (v1, 0)], tt)
    n_ctx = ctx_len // ts
    n_all = nt // ts
    fwd = lambda g: (g, 0)
    bwd = lambda g: (jnp.where(g < n_ctx, n_ctx - 1 - g, n_all - 1 - g + n_ctx), 0)
    kin = lambda m: pl.BlockSpec((ts * RW_HD, lanes), m)
    yout = lambda m: pl.BlockSpec((ts * RW_VH, lanes), m)
    yshape = jax.ShapeDtypeStruct((nt * RW_VH, lanes), F32)
    yf, yb = pl.pallas_call(
        functools.partial(_rw_scan_kernel, ts=ts, lanes=lanes),
        grid=(n_all,),
        in_specs=[kin(fwd), kin(fwd), kin(bwd), kin(bwd)],
        out_specs=[yout(fwd), yout(bwd)],
        out_shape=[yshape, yshape],
        scratch_shapes=[pltpu.VMEM((RW_VH, RW_HD, lanes), F32), pltpu.VMEM((2, 6, RW_HD, lanes), F32)],
        compiler_params=_params(("arbitrary",)),
    )(fa, fc, ba, fc)
    nat = jax.ShapeDtypeStruct((nb, RW_W, nt), F32)
    return pl.pallas_call(
        functools.partial(_from_chains_kernel, nb=nb, tt=tt),
        grid=(nt // tt,),
        in_specs=[pl.BlockSpec((tt, RW_VH, lanes), lambda i: (i, 0, 0))] * 2,
        out_specs=[pl.BlockSpec((nb, RW_W, tt), lambda i: (0, 0, i))] * 2,
        out_shape=[nat, nat],
        scratch_shapes=[pltpu.VMEM((nb * RW_H, RW_HD, tt), F32)],
        compiler_params=_params(("parallel",)),
    )(yf.reshape(nt, RW_VH, lanes), yb.reshape(nt, RW_VH, lanes))


def _outproj_kernel(x_ref, mod_ref, ys_ref, at_ref, wf_ref, wb_ref, bon_ref, g_ref,
                    gw_ref, gb_ref, sg_ref, lg_ref, lb_ref, avg_ref, wo_ref, o_ref, *, tm, ctx_len):
    j = pl.program_id(1)
    row = j * tm + lax.broadcasted_iota(jnp.int32, (tm, 1), 0)
    is_ctx = row < ctx_len
    a = jax.nn.gelu(ys_ref[0])
    o1 = _rms(a * _sigmoid(_dot_x3(a, gw_ref[...]) + gb_ref[...])) * sg_ref[...]
    wkv = (wf_ref[0] + wb_ref[0]).T
    avg = avg_ref[...]
    cen = wkv - _seg_dot(wkv, avg)
    var = _seg_dot(cen * cen, avg)
    o3 = (cen * lax.rsqrt(var + RW_LN_EPS) * lg_ref[...] + lb_ref[...] + bon_ref[0]) * g_ref[0]
    o = (_dot(o1.astype(BF16), wo_ref[0:S5_W, :])
         + _dot(at_ref[0].astype(BF16), wo_ref[S5_W:S5_W + ATT_W, :])
         + _dot(o3.astype(BF16), wo_ref[S5_W + ATT_W:, :]))
    o_ref[0] = x_ref[0] + _pick_mod(mod_ref[0], is_ctx, 2) * o


def _outproj(x, mod, ys5, att, wkv_f, wkv_b, bonus, g, glu_w, glu_b, s5_out_g, ln_g, ln_b, w_out, tm, ctx_len):
    nb, nt, d = x.shape
    row = lambda a: a.reshape(1, -1)
    avg = _block_diag_mean(RW_W, RW_HD)
    params = (glu_w, row(glu_b), row(s5_out_g), row(ln_g), row(ln_b), avg, w_out.astype(BF16))
    full = lambda a: pl.BlockSpec(a.shape, lambda b, j: (0,) * a.ndim)
    tok = lambda w: pl.BlockSpec((1, tm, w), lambda b, j: (b, j, 0))
    cm = pl.BlockSpec((1, RW_W, tm), lambda b, j: (b, 0, j))
    return pl.pallas_call(
        functools.partial(_outproj_kernel, tm=tm, ctx_len=ctx_len),
        grid=(nb, nt // tm),
        in_specs=[tok(d), pl.BlockSpec((1, 2, N_MOD * d), lambda b, j: (b, 0, 0)),
                  tok(S5_W), tok(ATT_W), cm, cm, tok(RW_W), tok(RW_W)] + [full(a) for a in params],
        out_specs=tok(d),
        out_shape=jax.ShapeDtypeStruct(x.shape, F32),
        compiler_params=_params(("parallel", "parallel")),
    )(x, mod, ys5, att, wkv_f, wkv_b, bonus, g, *params)


FFN_HALO = 16
FFN_TF = 256


def _ffn_kernel(x_ref, xp_ref, xn_ref, mod_ref, g_ref, upg_ref, upv_ref, cwg_ref, cwv_ref, cbg_ref, cbv_ref,
                dn_ref, o_ref, hs_ref, acc_ref, *, tm, ctx_len, nt):
    j = pl.program_id(1)
    f = pl.program_id(2)
    hl, tf = FFN_HALO, FFN_TF
    mod = mod_ref[0]

    def adaln(xv, first_row):
        rows = first_row + lax.broadcasted_iota(jnp.int32, (xv.shape[0], 1), 0)
        is_ctx = rows < ctx_len
        h = _rms(xv) * g_ref[...]
        return (h * (1.0 + _pick_mod(mod, is_ctx, 4)) + _pick_mod(mod, is_ctx, 3)).astype(BF16)

    @pl.when(f == 0)
    def _():
        hs_ref[0:hl, :] = adaln(xp_ref[0], j * tm - hl)
        hs_ref[hl:hl + tm, :] = adaln(x_ref[0], j * tm)
        hs_ref[hl + tm:, :] = adaln(xn_ref[0], (j + 1) * tm)
        acc_ref[...] = jnp.zeros_like(acc_ref)

    n_ext = tm + 2 * hl
    row = j * tm + lax.broadcasted_iota(jnp.int32, (tm, 1), 0)
    no_prev = (row == 0) | (row == ctx_len)
    no_next = (row == ctx_len - 1) | (row == nt - 1)
    hs = hs_ref[...]

    def conv(up_ref, cw_ref, cb_ref):
        u = _dot(hs, up_ref[...])
        prev = jnp.where(no_prev, 0.0, pltpu.roll(u, 1, axis=0)[hl:hl + tm])
        nxt = jnp.where(no_next, 0.0, pltpu.roll(u, n_ext - 1, axis=0)[hl:hl + tm])
        cw = cw_ref[...]
        return cw[0:1] * prev + cw[1:2] * u[hl:hl + tm] + cw[2:3] * nxt + cb_ref[...]

    half = 0.5 * conv(upg_ref, cwg_ref, cbg_ref)
    act = (half + half * jnp.tanh(half)) * conv(upv_ref, cwv_ref, cbv_ref)
    acc_ref[...] += _dot(act.astype(BF16), dn_ref[...])

    @pl.when(f == pl.num_programs(2) - 1)
    def _():
        rows = j * tm + lax.broadcasted_iota(jnp.int32, (tm, 1), 0)
        o_ref[0] = x_ref[0] + _pick_mod(mod, rows < ctx_len, 5) * acc_ref[...]


def _conv_ffn(x, mod, g, up, conv_w, conv_b, down, tm, ctx_len):
    nb, nt, d = x.shape
    tf, hl = FFN_TF, FFN_HALO
    nf = D_FF // tf
    upb = up.astype(BF16)
    cb = conv_b.reshape(1, -1)
    nh = nt // hl
    cols = lambda rows: [pl.BlockSpec((rows, tf), lambda b, j, f: (0, f)),
                         pl.BlockSpec((rows, tf), lambda b, j, f: (0, nf + f))]
    return pl.pallas_call(
        functools.partial(_ffn_kernel, tm=tm, ctx_len=ctx_len, nt=nt),
        grid=(nb, nt // tm, nf),
        in_specs=[pl.BlockSpec((1, tm, d), lambda b, j, f: (b, j, 0)),
                  pl.BlockSpec((1, hl, d), lambda b, j, f: (b, jnp.maximum(j * (tm // hl) - 1, 0), 0)),
                  pl.BlockSpec((1, hl, d), lambda b, j, f: (b, jnp.minimum((j + 1) * (tm // hl), nh - 1), 0)),
                  pl.BlockSpec((1, 2, N_MOD * d), lambda b, j, f: (b, 0, 0)),
                  pl.BlockSpec((1, d), lambda b, j, f: (0, 0))] + cols(d) + cols(3) + cols(1) + [
                  pl.BlockSpec((tf, d), lambda b, j, f: (f, 0))],
        out_specs=pl.BlockSpec((1, tm, d), lambda b, j, f: (b, j, 0)),
        out_shape=jax.ShapeDtypeStruct(x.shape, F32),
        scratch_shapes=[pltpu.VMEM((tm + 2 * hl, d), BF16), pltpu.VMEM((tm, d), F32)],
        compiler_params=_params(("parallel", "parallel", "arbitrary")),
    )(x, x, x, mod, g.reshape(1, d), upb, upb, conv_w, conv_w, cb, cb, down.astype(BF16))


def _final_kernel(x_ref, g_ref, o_ref):
    o_ref[0] = _rms(x_ref[0]) * g_ref[...]


def _final_norm(x, g, ctx_len, tm):
    nb, nt, d = x.shape
    off = ctx_len // tm
    return pl.pallas_call(
        _final_kernel,
        grid=(nb, (nt - ctx_len) // tm),
        in_specs=[pl.BlockSpec((1, tm, d), lambda b, j: (b, j + off, 0)),
                  pl.BlockSpec((1, d), lambda b, j: (0, 0))],
        out_specs=pl.BlockSpec((1, tm, d), lambda b, j: (b, j, 0)),
        out_shape=jax.ShapeDtypeStruct((nb, nt - ctx_len, d), F32),
        compiler_params=_params(("parallel", "parallel")),
    )(x, g.reshape(1, d))


def _forward(p, *, tm, tmf, tq, ts, tt, tfin):
    x, ctx = p["x"], p["ctx"]
    ctx_len = ctx.shape[1]
    depth = p["mod_w"].shape[0]
    xa = jnp.concatenate([ctx, x], axis=1)
    mods = _modulations(p["c"], p["c_ctx"], p["mod_w"], p["mod_b"])
    for l in range(depth):
        mod = mods[l]
        zs5, zq, zkv, zrw = _inproj(xa, mod, p["norm1_g"][l].reshape(1, -1), p["w_in"][l], tm, ctx_len)
        ops = _s5_operators(p["s5_a_re"][l], p["s5_a_im"][l], p["s5_log_step"][l], p["s5_b_re"][l],
                            p["s5_b_im"][l], p["s5_c_re"][l], p["s5_c_im"][l], p["s5_d"][l])
        ys5 = _s5_scan(zs5, ops, ctx_len)
        att = _attention(zq, zkv, p["att_qn_g"][l], p["att_kn_g"][l], p["att_out_g"][l], tq, ctx_len)
        r, v, a, w, b, kd, g, bonus = _rw_prep(zrw, p["rw_mu"][l], p["rw_w0"][l], p["rw_w2"][l], p["rw_a0"][l],
                                                 p["rw_a2"][l], p["rw_g2"][l], p["rw_k_k"][l], p["rw_k_a"][l],
                                                 p["rw_r_k"][l], tm, ctx_len)
        wkv_f, wkv_b = _rw_scan(r, v, a, w, b, kd, ts, tt, ctx_len)
        xa = _outproj(xa, mod, ys5, att, wkv_f, wkv_b, bonus, g, p["s5_glu_w"][l], p["s5_glu_b"][l],
                      p["s5_out_g"][l], p["rw_ln_g"][l], p["rw_ln_b"][l], p["w_out"][l], tm, ctx_len)
        xa = _conv_ffn(xa, mod, p["norm2_g"][l], p["ffn_up"][l], p["ffn_conv_w"][l], p["ffn_conv_b"][l],
                       p["ffn_down"][l], tmf, ctx_len)
    return _final_norm(xa, p["final_g"], ctx_len, tfin)


_ARG_NAMES = ("x c ctx c_ctx norm1_g norm2_g mod_w mod_b w_in w_out s5_a_re s5_a_im s5_log_step s5_b_re s5_b_im "
              "s5_c_re s5_c_im s5_d s5_glu_w s5_glu_b s5_out_g att_qn_g att_kn_g att_out_g rw_mu rw_w0 rw_w2 "
              "rw_a0 rw_a2 rw_g2 rw_k_k rw_k_a rw_r_k rw_ln_g rw_ln_b ffn_up ffn_conv_w ffn_conv_b ffn_down "
              "final_g").split()


def kernel(x, c, ctx, c_ctx, norm1_g, norm2_g, mod_w, mod_b, w_in, w_out, s5_a_re, s5_a_im, s5_log_step, s5_b_re, s5_b_im, s5_c_re, s5_c_im, s5_d, s5_glu_w, s5_glu_b, s5_out_g, att_qn_g, att_kn_g, att_out_g, rw_mu, rw_w0, rw_w2, rw_a0, rw_a2, rw_g2, rw_k_k, rw_k_a, rw_r_k, rw_ln_g, rw_ln_b, ffn_up, ffn_conv_w, ffn_conv_b, ffn_down, final_g):
    args = (x, c, ctx, c_ctx, norm1_g, norm2_g, mod_w, mod_b, w_in, w_out, s5_a_re, s5_a_im, s5_log_step, s5_b_re,
            s5_b_im, s5_c_re, s5_c_im, s5_d, s5_glu_w, s5_glu_b, s5_out_g, att_qn_g, att_kn_g, att_out_g, rw_mu,
            rw_w0, rw_w2, rw_a0, rw_a2, rw_g2, rw_k_k, rw_k_a, rw_r_k, rw_ln_g, rw_ln_b, ffn_up, ffn_conv_w,
            ffn_conv_b, ffn_down, final_g)
    return _forward(dict(zip(_ARG_NAMES, args)), tm=768, tmf=1152, tq=256, ts=64, tt=128, tfin=256)
```

```python
import functools
import math

import jax
import jax.numpy as jnp
from jax import lax
from jax.experimental import pallas as pl
from jax.experimental.pallas import tpu as pltpu

F32 = jnp.float32
BF16 = jnp.bfloat16
HI = lax.Precision.HIGHEST

D_MODEL = 1024
GRID_W = 64
S5_W = 256
S5_GH = 16
S5_G = 16
S5_P = 64
HEAD_DIM = 64
ATT_W = 512
N_Q = 8
GQA_REP = 4
N_KV = 2
ATT_KV = 128
ROPE_PAIRS = 16
ROPE_THETA = 10000.0
RW_W = 256
RW_HD = 64
RW_H = 4
LORA_W = 32
LORA_A = 32
LORA_G = 64
RW_IN = 960
RW_PAD = 1024
RW_LN_EPS = 64e-5
D_FF = 2816
N_MOD = 6
RMS_EPS = 1e-6

S5_T = 32
S5_CB = 8
V7X_VMEM_LIMIT = 56 * 1024 * 1024


def _params(sem, vmem=V7X_VMEM_LIMIT):
    return pltpu.CompilerParams(dimension_semantics=sem, vmem_limit_bytes=vmem)


def _rms(x):
    return x * lax.rsqrt(jnp.mean(x * x, axis=-1, keepdims=True) + RMS_EPS)


def _sigmoid(x):
    return 1.0 / (1.0 + jnp.exp(-x))


def _dot(a, b, precision=None):
    return jnp.dot(a, b, precision=precision, preferred_element_type=F32)


def _split2(x):
    hi = x.astype(BF16)
    return hi, (x - hi.astype(F32)).astype(BF16)


def _dot_x3(x, w):
    xh, xl = _split2(x)
    wh, wl = _split2(w)
    return _dot(jnp.concatenate([xh, xl, xh], axis=1), jnp.concatenate([wh, wh, wl], axis=0))


def _seg_dot(x, m):
    xh, xl = _split2(x)
    mb = m.astype(BF16)
    return _dot(jnp.concatenate([xh, xl], axis=1), jnp.concatenate([mb, mb], axis=0))


def _block_diag_mean(width, seg):
    i = jnp.arange(width)
    return ((i[:, None] // seg) == (i[None, :] // seg)).astype(F32) / seg


def _mod_kernel(c_ref, w_ref, b_ref, o_ref):
    c = c_ref[...]
    s = c * _sigmoid(c)
    o_ref[0] = _dot(s, w_ref[0], HI) + b_ref[0]


def _modulations(c, c_ctx, mod_w, mod_b):
    depth, d, n = mod_w.shape
    nb = c.shape[0]
    rows = ((nb + 1 + 7) // 8) * 8
    cc = jnp.zeros((rows, d), F32).at[:nb].set(c).at[nb].set(c_ctx)
    tn = n // 4
    out = pl.pallas_call(
        _mod_kernel,
        grid=(depth, n // tn),
        in_specs=[pl.BlockSpec((rows, d), lambda l, j: (0, 0)),
                  pl.BlockSpec((1, d, tn), lambda l, j: (l, 0, j)),
                  pl.BlockSpec((1, 1, tn), lambda l, j: (l, 0, j))],
        out_specs=pl.BlockSpec((1, rows, tn), lambda l, j: (l, 0, j)),
        out_shape=jax.ShapeDtypeStruct((depth, rows, n), F32),
        compiler_params=_params(("parallel", "parallel")),
    )(cc, mod_w, mod_b.reshape(depth, 1, n))
    lat = out[:, :nb]
    ctx = jnp.broadcast_to(out[:, nb:nb + 1], lat.shape)
    return jnp.stack([lat, ctx], axis=2)


def _pick_mod(mod, is_ctx, idx):
    lo = idx * D_MODEL
    return jnp.where(is_ctx, mod[1:2, lo:lo + D_MODEL], mod[0:1, lo:lo + D_MODEL])


def _inproj_kernel(x_ref, mod_ref, g_ref, ws_ref, wq_ref, wkv_ref, wr_ref,
                   s5_ref, q_ref, kv_ref, rw_ref, *, tm, ctx_len):
    j = pl.program_id(1)
    x = x_ref[0]
    row = j * tm + lax.broadcasted_iota(jnp.int32, (tm, 1), 0)
    is_ctx = row < ctx_len
    mod = mod_ref[0]
    h = _rms(x) * g_ref[...]
    h = h * (1.0 + _pick_mod(mod, is_ctx, 1)) + _pick_mod(mod, is_ctx, 0)
    hb = h.astype(BF16)
    s5_ref[0] = _dot(hb, ws_ref[...])
    q_ref[0] = _dot(hb, wq_ref[...])
    kv_ref[0] = _dot(hb, wkv_ref[...])
    rw_ref[0] = _dot(hb, wr_ref[...])


def _inproj(x, mod, g, w_in, tm, ctx_len):
    nb, nt, d = x.shape
    wb = w_in.astype(BF16)
    ws = wb[:, :S5_W]
    wq = wb[:, S5_W:S5_W + ATT_W]
    wkv = wb[:, S5_W + ATT_W:S5_W + ATT_W + 2 * ATT_KV]
    wr = jnp.pad(wb[:, S5_W + ATT_W + 2 * ATT_KV:], ((0, 0), (0, RW_PAD - RW_IN)))
    full = lambda a: pl.BlockSpec(a.shape, lambda b, j: (0,) * a.ndim)
    tok = lambda w: pl.BlockSpec((1, tm, w), lambda b, j: (b, j, 0))
    return pl.pallas_call(
        functools.partial(_inproj_kernel, tm=tm, ctx_len=ctx_len),
        grid=(nb, nt // tm),
        in_specs=[tok(d), pl.BlockSpec((1, 2, N_MOD * d), lambda b, j: (b, 0, 0)),
                  full(g), full(ws), full(wq), full(wkv), full(wr)],
        out_specs=[tok(S5_W), tok(ATT_W), tok(2 * ATT_KV), tok(RW_PAD)],
        out_shape=[jax.ShapeDtypeStruct((nb, nt, w), F32) for w in (S5_W, ATT_W, 2 * ATT_KV, RW_PAD)],
        compiler_params=_params(("parallel", "parallel")),
    )(x, mod, g, ws, wq, wkv, wr)


def _s5_operators(a_re, a_im, log_step, b_re, b_im, c_re, c_im, d):
    T = S5_T
    lam = lax.complex(a_re.astype(F32), a_im.astype(F32))
    step = jnp.exp(log_step.astype(F32))[..., None]
    lam_bar = jnp.exp(lam * step)
    bmat = lax.complex(b_re.astype(F32), b_im.astype(F32))
    b_bar = ((lam_bar - 1.0) / lam)[..., None] * bmat
    cmat = lax.complex(c_re.astype(F32), c_im.astype(F32))
    n = jnp.arange(T + 1, dtype=F32)
    pw = jnp.exp((lam * step)[..., None] * n)
    cp = jnp.swapaxes(cmat, -1, -2)[:, :, :, None, :] * pw[..., None]
    bp = jnp.transpose(pw, (0, 1, 3, 2))[:, :, :, None, :] * jnp.transpose(b_bar, (0, 1, 3, 2))[:, :, None]
    G, P, GH = S5_G, S5_P, S5_GH
    W = T * GH

    def ri_rows(z):
        return jnp.concatenate([jnp.real(z), -jnp.imag(z)], axis=1)

    def ri_cols(z):
        return jnp.concatenate([jnp.real(z), jnp.imag(z)], axis=2)

    bt = jnp.stack([jnp.concatenate([jnp.real(jnp.swapaxes(b_bar[di], -1, -2)),
                                     jnp.imag(jnp.swapaxes(b_bar[di], -1, -2))], axis=-1)
                    for di in range(2)], axis=1)
    cpk0 = ri_rows(cp[0, :, :, :T].reshape(G, P, W))
    cpk1 = ri_rows(cp[1, :, :, :T][:, :, ::-1].reshape(G, P, W))
    cc = jnp.concatenate([ri_rows(cp[0, :, :, 1:].reshape(G, P, W)),
                          ri_rows(cp[1, :, :, 1:][:, :, ::-1].reshape(G, P, W))], axis=1)
    bc = jnp.concatenate([ri_cols(bp[0, :, :T][:, ::-1].reshape(G, W, P)),
                          ri_cols(bp[1, :, :T].reshape(G, W, P))], axis=2)
    lt = pw[..., T]
    lam_rows = jnp.stack([jnp.concatenate([jnp.real(lt[0]), jnp.real(lt[0])], -1),
                          jnp.concatenate([-jnp.imag(lt[0]), jnp.imag(lt[0])], -1),
                          jnp.concatenate([jnp.real(lt[1]), jnp.real(lt[1])], -1),
                          jnp.concatenate([-jnp.imag(lt[1]), jnp.imag(lt[1])], -1)], axis=1)
    dvec = jnp.tile(d.astype(F32).reshape(G, 1, GH), (1, 1, T))
    return bt, cpk0, cpk1, bc, cc, lam_rows, dvec


def _s5_kernel(u_ref, bt_ref, cp0_ref, cp1_ref, bc_ref, cc_ref, lam_ref, d_ref, y_ref,
               kt_ref, spf_ref, spb_ref, ef_ref, eb_ref, *, n_ctx, n_chunks, nb):
    T, GH, P2 = S5_T, S5_GH, 2 * S5_P
    W = T * GH
    krow0 = _dot(bt_ref[0, 0], cp0_ref[0], HI)
    krow1 = _dot(bt_ref[0, 1], cp1_ref[0], HI)
    lane = lax.broadcasted_iota(jnp.int32, (GH, W), 1)
    for s in range(T):
        f = krow0 if s == 0 else pltpu.roll(krow0, s * GH, axis=1)
        f = jnp.where(lane >= s * GH, f, 0.0)
        sh = (T - 1 - s) * GH
        b = krow1 if sh == 0 else pltpu.roll(krow1, W - sh, axis=1)
        b = jnp.where(lane < (s + 1) * GH, b, 0.0)
        kt_ref[s * GH:(s + 1) * GH, :] = f + b
    u = u_ref[0]
    y = u * d_ref[0] + _dot_x3(u, kt_ref[...])
    e = _dot_x3(u, bc_ref[0])
    ef_ref[...] = e[:, :P2]
    eb_ref[...] = e[:, P2:]
    lam = lam_ref[0]

    def run(order, e_ref, sp_ref, ra, rb):
        s = jnp.zeros((nb, P2), F32)
        for c in order:
            rows = pl.ds((c // S5_CB) * S5_CB * nb + c % S5_CB, nb, stride=S5_CB)
            sp_ref[rows, :] = s
            s = ra * s + rb * pltpu.roll(s, S5_P, axis=1) + e_ref[rows, :]

    run(list(range(n_chunks)), ef_ref, spf_ref, lam[0:1], lam[1:2])
    run(list(range(n_ctx - 1, -1, -1)) + list(range(n_chunks - 1, n_ctx - 1, -1)), eb_ref, spb_ref,
        lam[2:3], lam[3:4])
    sp = jnp.concatenate([spf_ref[...], spb_ref[...]], axis=1)
    y_ref[0] = y + _dot_x3(sp, cc_ref[0])


S5_GL = 128 // S5_GH


def _s5_in_kernel(z_ref, u_ref, *, nb):
    T, GH, CB = S5_T, S5_GH, S5_CB

    def one_batch(b, carry):
        rows = pl.ds(pl.multiple_of(b * CB, CB), CB)
        for t in range(T):
            zt = z_ref[b, pl.ds(t, CB, stride=T), :]
            for g in range(S5_GL):
                u_ref[g, rows, t * GH:(t + 1) * GH] = zt[:, g * GH:(g + 1) * GH]
        return carry

    lax.fori_loop(0, nb, one_batch, 0)


def _s5_out_kernel(y_ref, o_ref, row_ref, *, nb):
    T, GH, CB = S5_T, S5_GH, S5_CB

    def one_batch(b, carry):
        rows = pl.ds(pl.multiple_of(b * CB, CB), CB)
        for t in range(T):
            for g in range(S5_GL):
                row_ref[:, g * GH:(g + 1) * GH] = y_ref[g, rows, t * GH:(t + 1) * GH]
            o_ref[b, pl.ds(t, CB, stride=T), :] = row_ref[...]
        return carry

    lax.fori_loop(0, nb, one_batch, 0)


def _s5_scan(zs5, ops, ctx_len):
    nb, nt, _ = zs5.shape
    T, G, GH, CB, GL = S5_T, S5_G, S5_GH, S5_CB, S5_GL
    W = T * GH
    n_chunks = nt // T
    n_blocks = n_chunks // CB
    rows = n_chunks * nb
    relayout_grid = (n_blocks, G // GL)
    nat = pl.BlockSpec((nb, CB * T, GL * GH), lambda i, h: (0, i, h))
    chunked = pl.BlockSpec((GL, nb * CB, W), lambda i, h: (h, i, 0))
    u = pl.pallas_call(
        functools.partial(_s5_in_kernel, nb=nb),
        grid=relayout_grid,
        in_specs=[nat],
        out_specs=chunked,
        out_shape=jax.ShapeDtypeStruct((G, rows, W), F32),
        compiler_params=_params(("parallel", "parallel")),
    )(zs5)
    bt, cpk0, cpk1, bc, cc, lam_rows, dvec = ops
    per_g = lambda a: pl.BlockSpec((1,) + a.shape[1:], lambda g: (g,) + (0,) * (a.ndim - 1))
    y = pl.pallas_call(
        functools.partial(_s5_kernel, n_ctx=ctx_len // T, n_chunks=n_chunks, nb=nb),
        grid=(G,),
        in_specs=[per_g(u), per_g(bt), per_g(cpk0), per_g(cpk1), per_g(bc), per_g(cc),
                  per_g(lam_rows), per_g(dvec)],
        out_specs=per_g(u),
        out_shape=jax.ShapeDtypeStruct(u.shape, F32),
        scratch_shapes=[pltpu.VMEM((W, W), F32)] + [pltpu.VMEM((rows, 2 * S5_P), F32)] * 4,
        compiler_params=_params(("parallel",)),
    )(u, bt, cpk0, cpk1, bc, cc, lam_rows, dvec)
    return pl.pallas_call(
        functools.partial(_s5_out_kernel, nb=nb),
        grid=relayout_grid,
        in_specs=[chunked],
        out_specs=nat,
        out_shape=jax.ShapeDtypeStruct(zs5.shape, F32),
        scratch_shapes=[pltpu.VMEM((CB, GL * GH), F32)],
        compiler_params=_params(("parallel", "parallel")),
    )(y)


def _rope_tables(nt, ctx_len, width):
    tok = jnp.arange(nt - ctx_len, dtype=jnp.int32)
    pos_row = (tok // GRID_W).astype(F32)
    pos_col = (tok % GRID_W).astype(F32)
    inv = ROPE_THETA ** (-jnp.arange(ROPE_PAIRS, dtype=F32) / ROPE_PAIRS)
    ang_r = pos_row[:, None] * inv
    ang_c = pos_col[:, None] * inv
    ang = jnp.concatenate([ang_r, ang_r, ang_c, ang_c], axis=1)
    sign = jnp.tile(jnp.concatenate([-jnp.ones(ROPE_PAIRS, F32), jnp.ones(ROPE_PAIRS, F32)]), 2)
    cos = jnp.concatenate([jnp.ones((ctx_len, HEAD_DIM), F32), jnp.cos(ang)], axis=0)
    sin = jnp.concatenate([jnp.zeros((ctx_len, HEAD_DIM), F32), jnp.sin(ang) * sign], axis=0)
    reps = width // HEAD_DIM
    return jnp.tile(cos, (1, reps)), jnp.tile(sin, (1, reps))


def _head_norm_rope(t, pmat, g, cos, sin):
    width = t.shape[-1]
    ms = _seg_dot(t * t, pmat)
    tn = t * lax.rsqrt(ms + RMS_EPS) * g
    lane = lax.broadcasted_iota(jnp.int32, tn.shape, 1)
    first = (lane % (2 * ROPE_PAIRS)) < ROPE_PAIRS
    partner = jnp.where(first, pltpu.roll(tn, width - ROPE_PAIRS, axis=1), pltpu.roll(tn, ROPE_PAIRS, axis=1))
    return tn * cos + partner * sin


def _attn_kernel(q_ref, kv_ref, cq_ref, sq_ref, ck_ref, sk_ref, qg_ref, kg_ref, og_ref, pq_ref, pk_ref,
                 o_ref, kn_ref, vb_ref, *, tq, ctx_len, nk):
    j = pl.program_id(1)

    @pl.when(j == 0)
    def _():
        kv = kv_ref[0]
        kr = _head_norm_rope(kv[:, :ATT_KV], pk_ref[...], kg_ref[...], ck_ref[...], sk_ref[...])
        ones = jnp.ones((kv.shape[0], HEAD_DIM), BF16)
        for g in range(N_KV):
            kn_ref[g] = kr[:, g * HEAD_DIM:(g + 1) * HEAD_DIM].astype(BF16)
            vb_ref[g, :, :HEAD_DIM] = kv[:, ATT_KV + g * HEAD_DIM:ATT_KV + (g + 1) * HEAD_DIM].astype(BF16)
            vb_ref[g, :, HEAD_DIM:] = ones

    qr = _head_norm_rope(q_ref[0], pq_ref[...], qg_ref[...], cq_ref[...], sq_ref[...]) * (
        HEAD_DIM ** -0.5 * math.log2(math.e))

    def attend(r0, r1, n_keys):
        outs = []
        for h in range(N_Q):
            g = h // GQA_REP
            qh = qr[r0:r1, h * HEAD_DIM:(h + 1) * HEAD_DIM].astype(BF16)
            s = lax.dot_general(qh, kn_ref[g, :n_keys, :], (((1,), (1,)), ((), ())),
                                preferred_element_type=F32)
            p = jnp.exp2(s - jnp.max(s, axis=-1, keepdims=True))
            ov = _dot(p.astype(BF16), vb_ref[g, :n_keys, :])
            outs.append(ov[:, :HEAD_DIM] / ov[:, HEAD_DIM:])
        o = jnp.concatenate(outs, axis=1)
        o_ref[0, r0:r1, :] = _rms(o) * og_ref[...]

    n_ctx_rows = min(ctx_len, tq)

    @pl.when(j * tq < ctx_len)
    def _():
        attend(0, n_ctx_rows, ctx_len)
        if n_ctx_rows < tq:
            attend(n_ctx_rows, tq, nk)

    @pl.when(j * tq >= ctx_len)
    def _():
        attend(0, tq, nk)


def _attention(zq, zkv, qn_g, kn_g, out_g, tq, ctx_len):
    nb, nt, _ = zq.shape
    cq, sq = _rope_tables(nt, ctx_len, ATT_W)
    ck, sk = cq[:, :ATT_KV], sq[:, :ATT_KV]
    qg = jnp.tile(qn_g, N_Q).reshape(1, ATT_W)
    kg = jnp.tile(kn_g, N_KV).reshape(1, ATT_KV)
    pq = _block_diag_mean(ATT_W, HEAD_DIM)
    pk = _block_diag_mean(ATT_KV, HEAD_DIM)
    full = lambda a: pl.BlockSpec(a.shape, lambda b, j: (0,) * a.ndim)
    return pl.pallas_call(
        functools.partial(_attn_kernel, tq=tq, ctx_len=ctx_len, nk=nt),
        grid=(nb, nt // tq),
        in_specs=[pl.BlockSpec((1, tq, ATT_W), lambda b, j: (b, j, 0)),
                  pl.BlockSpec((1, nt, 2 * ATT_KV), lambda b, j: (b, 0, 0)),
                  pl.BlockSpec((tq, ATT_W), lambda b, j: (j, 0)),
                  pl.BlockSpec((tq, ATT_W), lambda b, j: (j, 0)),
                  full(ck), full(sk), full(qg), full(kg),
                  pl.BlockSpec((1, ATT_W), lambda b, j: (0, 0)), full(pq), full(pk)],
        out_specs=pl.BlockSpec((1, tq, ATT_W), lambda b, j: (b, j, 0)),
        out_shape=jax.ShapeDtypeStruct((nb, nt, ATT_W), F32),
        scratch_shapes=[pltpu.VMEM((N_KV, nt, HEAD_DIM), BF16), pltpu.VMEM((N_KV, nt, 2 * HEAD_DIM), BF16)],
        compiler_params=_params(("parallel", "arbitrary")),
    )(zq, zkv, cq, sq, ck, sk, qg, kg, out_g.reshape(1, ATT_W), pq, pk)


def _rw_prep_kernel(z_ref, zp_ref, zn_ref, mu_ref, wl_ref, w0_ref, a0_ref, kk_ref, ka_ref, rk_ref, ones_ref,
                    r_ref, v_ref, a_ref, w_ref, b_ref, kd_ref, g_ref, bon_ref, *, tm, ctx_len, nt):
    j = pl.program_id(1)
    z = z_ref[0]
    row_l = lax.broadcasted_iota(jnp.int32, (tm, 1), 0)
    row = j * tm + row_l
    prev = jnp.where(row_l == 0, zp_ref[0, 7:8, :], pltpu.roll(z, 1, axis=0))
    prev = jnp.where((row == 0) | (row == ctx_len), 0.0, prev)
    nxt = jnp.where(row_l == tm - 1, zn_ref[0, 0:1, :], pltpu.roll(z, tm - 1, axis=0))
    nxt = jnp.where((row == ctx_len - 1) | (row == nt - 1), 0.0, nxt)
    zs = z + mu_ref[...] * (0.5 * (prev + nxt) - z)
    r = zs[:, 0:RW_W]
    k = zs[:, RW_W:2 * RW_W]
    v = zs[:, 2 * RW_W:3 * RW_W]
    lo = zs[:, 3 * RW_W:]
    lane = lax.broadcasted_iota(jnp.int32, lo.shape, 1)
    lo = jnp.where(lane < 2 * LORA_W, jnp.tanh(lo),
                   jnp.where(lane < 2 * LORA_W + 2 * LORA_A, lo, _sigmoid(lo)))
    proj = _dot_x3(lo, wl_ref[...])
    ones = ones_ref[...]
    kk = k * kk_ref[...]
    nrm = jnp.sqrt(_seg_dot(kk * kk, ones))
    kk = kk / jnp.maximum(nrm, 1e-12)
    r_ref[0] = r.T
    v_ref[0] = v.T
    a_ref[0] = (-kk).T
    g_ref[0] = proj[:, 4 * RW_W:5 * RW_W]
    bon = jnp.zeros_like(r)
    for di in range(2):
        pre = w0_ref[di:di + 1, :] + proj[:, di * RW_W:(di + 1) * RW_W]
        w_log = -(jnp.maximum(-pre, 0.0) + jnp.log(1.0 + jnp.exp(-jnp.abs(pre)))) - 0.5
        w_ref[di, 0] = jnp.exp(-jnp.exp(w_log)).T
        iclr = _sigmoid(a0_ref[di:di + 1, :] + proj[:, (2 + di) * RW_W:(3 + di) * RW_W])
        kd = k * (1.0 + (iclr - 1.0) * ka_ref[...])
        kd_ref[di, 0] = kd.T
        b_ref[di, 0] = (kk * iclr).T
        bon = bon + _seg_dot(r * kd * rk_ref[...], ones) * v
    bon_ref[0] = bon


def _rw_prep(zrw, mu, w0, w2, a0, a2, g2, k_k, k_a, r_k, tm, ctx_len):
    nb, nt, _ = zrw.shape
    wl = jnp.zeros((RW_PAD - 3 * RW_W, 5 * RW_W), F32)
    wl = wl.at[0:LORA_W, 0:RW_W].set(w2[0]).at[LORA_W:2 * LORA_W, RW_W:2 * RW_W].set(w2[1])
    o = 2 * LORA_W
    wl = wl.at[o:o + LORA_A, 2 * RW_W:3 * RW_W].set(a2[0]).at[o + LORA_A:o + 2 * LORA_A, 3 * RW_W:4 * RW_W].set(a2[1])
    o = 2 * LORA_W + 2 * LORA_A
    wl = wl.at[o:o + LORA_G, 4 * RW_W:5 * RW_W].set(g2)
    mu_p = jnp.pad(mu, (0, RW_PAD - RW_IN)).reshape(1, RW_PAD)
    ones = _block_diag_mean(RW_W, RW_HD) * RW_HD
    row = lambda a: a.reshape(1, RW_W)
    full = lambda a: pl.BlockSpec(a.shape, lambda b, j: (0,) * a.ndim)
    tok = pl.BlockSpec((1, tm, RW_W), lambda b, j: (b, j, 0))
    one = jax.ShapeDtypeStruct((nb, nt, RW_W), F32)
    cm = pl.BlockSpec((1, RW_W, tm), lambda b, j: (b, 0, j))
    cm2 = pl.BlockSpec((2, 1, RW_W, tm), lambda b, j: (0, b, 0, j))
    cm_one = jax.ShapeDtypeStruct((nb, RW_W, nt), F32)
    cm_two = jax.ShapeDtypeStruct((2, nb, RW_W, nt), F32)
    n8 = nt // 8
    args = (zrw, zrw, zrw, mu_p, wl, w0, a0, row(k_k), row(k_a), row(r_k), ones)
    return pl.pallas_call(
        functools.partial(_rw_prep_kernel, tm=tm, ctx_len=ctx_len, nt=nt),
        grid=(nb, nt // tm),
        in_specs=[pl.BlockSpec((1, tm, RW_PAD), lambda b, j: (b, j, 0)),
                  pl.BlockSpec((1, 8, RW_PAD), lambda b, j: (b, jnp.maximum(j * (tm // 8) - 1, 0), 0)),
                  pl.BlockSpec((1, 8, RW_PAD), lambda b, j: (b, jnp.minimum((j + 1) * (tm // 8), n8 - 1), 0)),
                  ] + [full(a) for a in args[3:]],
        out_specs=[cm, cm, cm, cm2, cm2, cm2, tok, tok],
        out_shape=[cm_one, cm_one, cm_one, cm_two, cm_two, cm_two, one, one],
        compiler_params=_params(("parallel", "parallel")),
    )(*args)


RW_VH = RW_HD // 2
RW_SLOTS = 4
RW_VGROUP = 16


def _to_chains_kernel(*refs, nb, tt):
    *x_refs, o_ref, scr_ref = refs
    for j, ref in enumerate(x_refs):
        for b in range(nb):
            xt = ref[0, b]
            for h in range(RW_H):
                scr_ref[(j * nb + b) * RW_H + h] = xt[h * RW_HD:(h + 1) * RW_HD, :]
    reps = RW_SLOTS // len(x_refs)
    for kg in range(RW_HD // 8):
        blk = jnp.swapaxes(scr_ref[:, kg * 8:(kg + 1) * 8, :], 0, 1)
        blk = jnp.concatenate([blk] * reps, axis=1)
        o_ref[:, kg * 8:(kg + 1) * 8, :] = jnp.swapaxes(jnp.swapaxes(blk, 1, 2), 0, 1)


def _to_chains(xs, tt):
    _, nb, _, nt = xs[0][0].shape
    lanes = RW_SLOTS * nb * RW_H
    spec = lambda di: pl.BlockSpec((1, nb, RW_W, tt), lambda i: (di, 0, 0, i))
    out = pl.pallas_call(
        functools.partial(_to_chains_kernel, nb=nb, tt=tt),
        grid=(nt // tt,),
        in_specs=[spec(di) for _, di in xs],
        out_specs=pl.BlockSpec((tt, RW_HD, lanes), lambda i: (i, 0, 0)),
        out_shape=jax.ShapeDtypeStruct((nt, RW_HD, lanes), F32),
        scratch_shapes=[pltpu.VMEM((len(xs) * nb * RW_H, RW_HD, tt), F32)],
        compiler_params=_params(("parallel",)),
    )(*[x for x, _ in xs])
    return out.reshape(nt * RW_HD, lanes)


def _rw_scan_kernel(fa_ref, fc_ref, ba_ref, bc_ref, yf_ref, yb_ref, s_ref, vec_ref, *, ts, lanes):
    ones = jnp.ones((RW_HD, lanes), F32)

    @pl.when(pl.program_id(0) == 0)
    def _():
        s_ref[...] = jnp.zeros_like(s_ref)
        vec_ref[1, 5] = ones

    p_end = vec_ref[1, 5]
    for vi in range(RW_VH):
        s_ref[vi] = s_ref[vi] * p_end
    vec_ref[1, 5] = ones

    lane = lax.broadcasted_iota(jnp.int32, (RW_HD, lanes), 1)
    q = lanes // 4
    even = (lane // q) % 2 == 0
    low = lane < 2 * q
    low_v = lax.broadcasted_iota(jnp.int32, (RW_VH, lanes), 1) < 2 * q

    def pairs(x):
        xs = pltpu.roll(x, 2 * q, axis=1)
        return jnp.where(low, x, xs), jnp.where(low, xs, x)

    def merge(f, b):
        return (jnp.where(even, f, pltpu.roll(b, q, axis=1)),
                jnp.where(even, pltpu.roll(f, lanes - q, axis=1), b))

    def prepare(i, slot):
        fr = pl.ds(pl.multiple_of(i * RW_HD, RW_HD), RW_HD)
        br = pl.ds(pl.multiple_of((ts - 1 - i) * RW_HD, RW_HD), RW_HD)
        f01, f23 = pairs(fa_ref[fr, :])
        b01, b23 = pairs(ba_ref[br, :])
        w, bt = merge(f01, b01)
        kd, a = merge(f23, b23)
        r, vk = merge(fc_ref[fr, :], bc_ref[br, :])
        p_prev = vec_ref[1 - slot, 5]
        p = p_prev * w
        inv = 1.0 / p
        vec_ref[slot, 0] = bt * inv
        vec_ref[slot, 1] = kd * inv
        vec_ref[slot, 2] = a * p_prev
        vec_ref[slot, 3] = r * p
        vec_ref[slot, 4, :RW_VH] = jnp.where(low_v, vk[:RW_VH], vk[RW_VH:])
        vec_ref[slot, 5] = p

    def all_sum8(x):
        x = x + pltpu.roll(x, 4, axis=0)
        x = x + pltpu.roll(x, 2, axis=0)
        return x + pltpu.roll(x, 1, axis=0)

    kblocks = [slice(kb * 8, (kb + 1) * 8) for kb in range(RW_HD // 8)]

    def update(i, slot):
        for v0 in range(0, RW_VH, RW_VGROUP):
            vis = range(v0, v0 + RW_VGROUP)
            acc = {}
            for kb in kblocks:
                a_kb = vec_ref[slot, 2, kb, :]
                for vi in vis:
                    prod = s_ref[vi, kb, :] * a_kb
                    acc[vi] = prod if vi not in acc else acc[vi] + prod
            sa = {vi: all_sum8(acc[vi]) for vi in vis}
            yacc = {}
            for kb in kblocks:
                b_kb, k_kb, r_kb = vec_ref[slot, 0, kb, :], vec_ref[slot, 1, kb, :], vec_ref[slot, 3, kb, :]
                for vi in vis:
                    s = s_ref[vi, kb, :] + sa[vi] * b_kb + vec_ref[slot, 4, vi:vi + 1, :] * k_kb
                    s_ref[vi, kb, :] = s
                    prod = s * r_kb
                    yacc[vi] = prod if vi not in yacc else yacc[vi] + prod
            for vi in vis:
                y = all_sum8(yacc[vi])[0:1]
                yf_ref[pl.ds(i * RW_VH + vi, 1), :] = y
                yb_ref[pl.ds((ts - 1 - i) * RW_VH + vi, 1), :] = y

    prepare(0, 0)

    def two_steps(h, carry):
        i = 2 * h
        prepare(i + 1, 1)
        update(i, 0)
        prepare(jnp.minimum(i + 2, ts - 1), 0)
        update(i + 1, 1)
        return carry

    lax.fori_loop(0, ts // 2, two_steps, 0)


def _from_chains_kernel(yf_ref, yb_ref, of_ref, ob_ref, scr_ref, *, nb, tt):
    q = nb * RW_H
    for d, (y_ref, o_ref) in enumerate(((yf_ref, of_ref), (yb_ref, ob_ref))):
        for vg in range(RW_VH // 8):
            blk = jnp.swapaxes(y_ref[:, vg * 8:(vg + 1) * 8, :], 0, 1)
            blk = jnp.swapaxes(jnp.swapaxes(blk, 1, 2), 0, 1)
            for vh in range(2):
                src = (vh * 2 + d) * q
                scr_ref[:, vh * RW_VH + vg * 8:vh * RW_VH + (vg + 1) * 8, :] = blk[src:src + q]
        for b in range(nb):
            o_ref[b] = scr_ref[b * RW_H:(b + 1) * RW_H].reshape(RW_W, tt)


def _rw_scan(r, v, a, w, b, kd, ts, tt, ctx_len):
    nb, _, nt = r.shape
    lanes = RW_SLOTS * nb * RW_H
    one = lambda x: x.reshape(1, nb, RW_W, nt)
    r1, v1, a1 = one(r), one(v), one(a)
    fa = _to_chains([(w, 0), (b, 0), (kd, 0), (a1, 0)], tt)
    ba = _to_chains([(w, 1), (b, 1), (kd, 1), (a1, 0)], tt)
    fc = _to_chains([(r1, 0), (v1, 0)], tt)
    n_ctx = ctx_len // ts
    n_all = nt // ts
    fwd = lambda g: (g, 0)
    bwd = lambda g: (jnp.where(g < n_ctx, n_ctx - 1 - g, n_all - 1 - g + n_ctx), 0)
    kin = lambda m: pl.BlockSpec((ts * RW_HD, lanes), m)
    yout = lambda m: pl.BlockSpec((ts * RW_VH, lanes), m)
    yshape = jax.ShapeDtypeStruct((nt * RW_VH, lanes), F32)
    yf, yb = pl.pallas_call(
        functools.partial(_rw_scan_kernel, ts=ts, lanes=lanes),
        grid=(n_all,),
        in_specs=[kin(fwd), kin(fwd), kin(bwd), kin(bwd)],
        out_specs=[yout(fwd), yout(bwd)],
        out_shape=[yshape, yshape],
        scratch_shapes=[pltpu.VMEM((RW_VH, RW_HD, lanes), F32), pltpu.VMEM((2, 6, RW_HD, lanes), F32)],
        compiler_params=_params(("arbitrary",)),
    )(fa, fc, ba, fc)
    nat = jax.ShapeDtypeStruct((nb, RW_W, nt), F32)
    return pl.pallas_call(
        functools.partial(_from_chains_kernel, nb=nb, tt=tt),
        grid=(nt // tt,),
        in_specs=[pl.BlockSpec((tt, RW_VH, lanes), lambda i: (i, 0, 0))] * 2,
        out_specs=[pl.BlockSpec((nb, RW_W, tt), lambda i: (0, 0, i))] * 2,
        out_shape=[nat, nat],
        scratch_shapes=[pltpu.VMEM((nb * RW_H, RW_HD, tt), F32)],
        compiler_params=_params(("parallel",)),
    )(yf.reshape(nt, RW_VH, lanes), yb.reshape(nt, RW_VH, lanes))


def _outproj_kernel(x_ref, mod_ref, ys_ref, at_ref, wf_ref, wb_ref, bon_ref, g_ref,
                    gw_ref, gb_ref, sg_ref, lg_ref, lb_ref, avg_ref, wo_ref, o_ref, *, tm, ctx_len):
    j = pl.program_id(1)
    row = j * tm + lax.broadcasted_iota(jnp.int32, (tm, 1), 0)
    is_ctx = row < ctx_len
    a = jax.nn.gelu(ys_ref[0])
    o1 = _rms(a * _sigmoid(_dot_x3(a, gw_ref[...]) + gb_ref[...])) * sg_ref[...]
    wkv = (wf_ref[0] + wb_ref[0]).T
    avg = avg_ref[...]
    cen = wkv - _seg_dot(wkv, avg)
    var = _seg_dot(cen * cen, avg)
    o3 = (cen * lax.rsqrt(var + RW_LN_EPS) * lg_ref[...] + lb_ref[...] + bon_ref[0]) * g_ref[0]
    o = (_dot(o1.astype(BF16), wo_ref[0:S5_W, :])
         + _dot(at_ref[0].astype(BF16), wo_ref[S5_W:S5_W + ATT_W, :])
         + _dot(o3.astype(BF16), wo_ref[S5_W + ATT_W:, :]))
    o_ref[0] = x_ref[0] + _pick_mod(mod_ref[0], is_ctx, 2) * o


def _outproj(x, mod, ys5, att, wkv_f, wkv_b, bonus, g, glu_w, glu_b, s5_out_g, ln_g, ln_b, w_out, tm, ctx_len):
    nb, nt, d = x.shape
    row = lambda a: a.reshape(1, -1)
    avg = _block_diag_mean(RW_W, RW_HD)
    params = (glu_w, row(glu_b), row(s5_out_g), row(ln_g), row(ln_b), avg, w_out.astype(BF16))
    full = lambda a: pl.BlockSpec(a.shape, lambda b, j: (0,) * a.ndim)
    tok = lambda w: pl.BlockSpec((1, tm, w), lambda b, j: (b, j, 0))
    cm = pl.BlockSpec((1, RW_W, tm), lambda b, j: (b, 0, j))
    return pl.pallas_call(
        functools.partial(_outproj_kernel, tm=tm, ctx_len=ctx_len),
        grid=(nb, nt // tm),
        in_specs=[tok(d), pl.BlockSpec((1, 2, N_MOD * d), lambda b, j: (b, 0, 0)),
                  tok(S5_W), tok(ATT_W), cm, cm, tok(RW_W), tok(RW_W)] + [full(a) for a in params],
        out_specs=tok(d),
        out_shape=jax.ShapeDtypeStruct(x.shape, F32),
        compiler_params=_params(("parallel", "parallel")),
    )(x, mod, ys5, att, wkv_f, wkv_b, bonus, g, *params)


FFN_HALO = 16
FFN_TF = 256


def _ffn_kernel(x_ref, xp_ref, xn_ref, mod_ref, g_ref, upg_ref, upv_ref, cwg_ref, cwv_ref, cbg_ref, cbv_ref,
                dn_ref, o_ref, hs_ref, acc_ref, *, tm, ctx_len, nt):
    j = pl.program_id(1)
    f = pl.program_id(2)
    hl, tf = FFN_HALO, FFN_TF
    mod = mod_ref[0]

    def adaln(xv, first_row):
        rows = first_row + lax.broadcasted_iota(jnp.int32, (xv.shape[0], 1), 0)
        is_ctx = rows < ctx_len
        h = _rms(xv) * g_ref[...]
        return (h * (1.0 + _pick_mod(mod, is_ctx, 4)) + _pick_mod(mod, is_ctx, 3)).astype(BF16)

    @pl.when(f == 0)
    def _():
        hs_ref[0:hl, :] = adaln(xp_ref[0], j * tm - hl)
        hs_ref[hl:hl + tm, :] = adaln(x_ref[0], j * tm)
        hs_ref[hl + tm:, :] = adaln(xn_ref[0], (j + 1) * tm)
        acc_ref[...] = jnp.zeros_like(acc_ref)

    n_ext = tm + 2 * hl
    row = j * tm + lax.broadcasted_iota(jnp.int32, (tm, 1), 0)
    no_prev = (row == 0) | (row == ctx_len)
    no_next = (row == ctx_len - 1) | (row == nt - 1)
    hs = hs_ref[...]

    def conv(up_ref, cw_ref, cb_ref):
        u = _dot(hs, up_ref[...])
        prev = jnp.where(no_prev, 0.0, pltpu.roll(u, 1, axis=0)[hl:hl + tm])
        nxt = jnp.where(no_next, 0.0, pltpu.roll(u, n_ext - 1, axis=0)[hl:hl + tm])
        cw = cw_ref[...]
        return cw[0:1] * prev + cw[1:2] * u[hl:hl + tm] + cw[2:3] * nxt + cb_ref[...]

    half = 0.5 * conv(upg_ref, cwg_ref, cbg_ref)
    act = (half + half * jnp.tanh(half)) * conv(upv_ref, cwv_ref, cbv_ref)
    acc_ref[...] += _dot(act.astype(BF16), dn_ref[...])

    @pl.when(f == pl.num_programs(2) - 1)
    def _():
        rows = j * tm + lax.broadcasted_iota(jnp.int32, (tm, 1), 0)
        o_ref[0] = x_ref[0] + _pick_mod(mod, rows < ctx_len, 5) * acc_ref[...]


def _conv_ffn(x, mod, g, up, conv_w, conv_b, down, tm, ctx_len):
    nb, nt, d = x.shape
    tf, hl = FFN_TF, FFN_HALO
    nf = D_FF // tf
    upb = up.astype(BF16)
    cb = conv_b.reshape(1, -1)
    nh = nt // hl
    cols = lambda rows: [pl.BlockSpec((rows, tf), lambda b, j, f: (0, f)),
                         pl.BlockSpec((rows, tf), lambda b, j, f: (0, nf + f))]
    return pl.pallas_call(
        functools.partial(_ffn_kernel, tm=tm, ctx_len=ctx_len, nt=nt),
        grid=(nb, nt // tm, nf),
        in_specs=[pl.BlockSpec((1, tm, d), lambda b, j, f: (b, j, 0)),
                  pl.BlockSpec((1, hl, d), lambda b, j, f: (b, jnp.maximum(j * (tm // hl) - 1, 0), 0)),
                  pl.BlockSpec((1, hl, d), lambda b, j, f: (b, jnp.minimum((j + 1) * (tm // hl), nh - 1), 0)),
                  pl.BlockSpec((1, 2, N_MOD * d), lambda b, j, f: (b, 0, 0)),
                  pl.BlockSpec((1, d), lambda b, j, f: (0, 0))] + cols(d) + cols(3) + cols(1) + [
                  pl.BlockSpec((tf, d), lambda b, j, f: (f, 0))],
        out_specs=pl.BlockSpec((1, tm, d), lambda b, j, f: (b, j, 0)),
        out_shape=jax.ShapeDtypeStruct(x.shape, F32),
        scratch_shapes=[pltpu.VMEM((tm + 2 * hl, d), BF16), pltpu.VMEM((tm, d), F32)],
        compiler_params=_params(("parallel", "parallel", "arbitrary")),
    )(x, x, x, mod, g.reshape(1, d), upb, upb, conv_w, conv_w, cb, cb, down.astype(BF16))


def _final_kernel(x_ref, g_ref, o_ref):
    o_ref[0] = _rms(x_ref[0]) * g_ref[...]


def _final_norm(x, g, ctx_len, tm):
    nb, nt, d = x.shape
    off = ctx_len // tm
    return pl.pallas_call(
        _final_kernel,
        grid=(nb, (nt - ctx_len) // tm),
        in_specs=[pl.BlockSpec((1, tm, d), lambda b, j: (b, j + off, 0)),
                  pl.BlockSpec((1, d), lambda b, j: (0, 0))],
        out_specs=pl.BlockSpec((1, tm, d), lambda b, j: (b, j, 0)),
        out_shape=jax.ShapeDtypeStruct((nb, nt - ctx_len, d), F32),
        compiler_params=_params(("parallel", "parallel")),
    )(x, g.reshape(1, d))


def _forward(p, *, tm, tmf, tq, ts, tt, tfin):
    x, ctx = p["x"], p["ctx"]
    ctx_len = ctx.shape[1]
    depth = p["mod_w"].shape[0]
    xa = jnp.concatenate([ctx, x], axis=1)
    mods = _modulations(p["c"], p["c_ctx"], p["mod_w"], p["mod_b"])
    for l in range(depth):
        mod = mods[l]
        zs5, zq, zkv, zrw = _inproj(xa, mod, p["norm1_g"][l].reshape(1, -1), p["w_in"][l], tm, ctx_len)
        ops = _s5_operators(p["s5_a_re"][l], p["s5_a_im"][l], p["s5_log_step"][l], p["s5_b_re"][l],
                            p["s5_b_im"][l], p["s5_c_re"][l], p["s5_c_im"][l], p["s5_d"][l])
        ys5 = _s5_scan(zs5, ops, ctx_len)
        att = _attention(zq, zkv, p["att_qn_g"][l], p["att_kn_g"][l], p["att_out_g"][l], tq, ctx_len)
        r, v, a, w, b, kd, g, bonus = _rw_prep(zrw, p["rw_mu"][l], p["rw_w0"][l], p["rw_w2"][l], p["rw_a0"][l],
                                                 p["rw_a2"][l], p["rw_g2"][l], p["rw_k_k"][l], p["rw_k_a"][l],
                                                 p["rw_r_k"][l], tm, ctx_len)
        wkv_f, wkv_b = _rw_scan(r, v, a, w, b, kd, ts, tt, ctx_len)
        xa = _outproj(xa, mod, ys5, att, wkv_f, wkv_b, bonus, g, p["s5_glu_w"][l], p["s5_glu_b"][l],
                      p["s5_out_g"][l], p["rw_ln_g"][l], p["rw_ln_b"][l], p["w_out"][l], tm, ctx_len)
        xa = _conv_ffn(xa, mod, p["norm2_g"][l], p["ffn_up"][l], p["ffn_conv_w"][l], p["ffn_conv_b"][l],
                       p["ffn_down"][l], tmf, ctx_len)
    return _final_norm(xa, p["final_g"], ctx_len, tfin)


_ARG_NAMES = ("x c ctx c_ctx norm1_g norm2_g mod_w mod_b w_in w_out s5_a_re s5_a_im s5_log_step s5_b_re s5_b_im "
              "s5_c_re s5_c_im s5_d s5_glu_w s5_glu_b s5_out_g att_qn_g att_kn_g att_out_g rw_mu rw_w0 rw_w2 "
              "rw_a0 rw_a2 rw_g2 rw_k_k rw_k_a rw_r_k rw_ln_g rw_ln_b ffn_up ffn_conv_w ffn_conv_b ffn_down "
              "final_g").split()


def kernel(x, c, ctx, c_ctx, norm1_g, norm2_g, mod_w, mod_b, w_in, w_out, s5_a_re, s5_a_im, s5_log_step, s5_b_re, s5_b_im, s5_c_re, s5_c_im, s5_d, s5_glu_w, s5_glu_b, s5_out_g, att_qn_g, att_kn_g, att_out_g, rw_mu, rw_w0, rw_w2, rw_a0, rw_a2, rw_g2, rw_k_k, rw_k_a, rw_r_k, rw_ln_g, rw_ln_b, ffn_up, ffn_conv_w, ffn_conv_b, ffn_down, final_g):
    args = (x, c, ctx, c_ctx, norm1_g, norm2_g, mod_w, mod_b, w_in, w_out, s5_a_re, s5_a_im, s5_log_step, s5_b_re,
            s5_b_im, s5_c_re, s5_c_im, s5_d, s5_glu_w, s5_glu_b, s5_out_g, att_qn_g, att_kn_g, att_out_g, rw_mu,
            rw_w0, rw_w2, rw_a0, rw_a2, rw_g2, rw_k_k, rw_k_a, rw_r_k, rw_ln_g, rw_ln_b, ffn_up, ffn_conv_w,
            ffn_conv_b, ffn_down, final_g)
    return _forward(dict(zip(_ARG_NAMES, args)), tm=768, tmf=1152, tq=768, ts=64, tt=128, tfin=256)
```
